```python
import jax, jax.numpy as jnp
from jax import lax
import numpy as np

D_MODEL = 1024
BATCH = 8
SEQ = 2048
DEPTH = 4

GRID_W = 64
CTX_LEN = 256
N_MIXERS = 3
N_WIN = (DEPTH + 2) // 3
N_GLB = (DEPTH + 1) // 3
N_GDN = DEPTH // 3

HEAD_DIM = 64
N_HEADS = D_MODEL // HEAD_DIM
N_KV_HEADS = 4
GROUP = N_HEADS // N_KV_HEADS
QKV_DIM = (N_HEADS + 2 * N_KV_HEADS) * HEAD_DIM
WINDOW = 128
Q_BLOCK = 128
ROPE_THETA = 10000.0
AXIS_DIM = HEAD_DIM // 2

GDN_HEADS = D_MODEL // 128
GDN_DK = 128
GDN_DV = 128
GDN_KW = GDN_HEADS * GDN_DK
GDN_VW = GDN_HEADS * GDN_DV
GDN_IN_DIM = 2 * GDN_KW + 2 * GDN_VW + 4 * GDN_HEADS
CONV_K = 5
CHUNK = 64

N_EXPERTS = 32
TOP_K = 4
D_EXPERT = D_MODEL
SWIGLU_LIMIT = 7.0
SWIGLU_ALPHA = 1.702
EXPERT_BLOCK = 128

NORM_EPS = 1e-6
NEG_INF = -1e30

kernel_name = 'hybrid_interleaved_flow_backbone'


def rms_norm(x, g):
    xf = x.astype(jnp.float32)
    y = xf * lax.rsqrt(jnp.mean(xf * xf, axis=-1, keepdims=True) + NORM_EPS)
    return (y * g.astype(jnp.float32)).astype(x.dtype)


def l2_norm(x):
    return x * lax.rsqrt(jnp.sum(x * x, axis=-1, keepdims=True) + NORM_EPS)


def adaln(cond, w_mod, b_mod):
    m = jax.nn.silu(cond) @ w_mod + b_mod
    return jnp.split(m[:, None, :], 6, axis=-1)


def axial_tables(length, dtype):
    rows = length // GRID_W
    row = jnp.broadcast_to(jnp.arange(rows)[:, None], (rows, GRID_W)).reshape(-1)
    col = jnp.broadcast_to(jnp.arange(GRID_W)[None, :], (rows, GRID_W)).reshape(-1)
    pos = jnp.stack([row, col], axis=-1).astype(jnp.float32)
    inv = ROPE_THETA ** (-jnp.arange(0, AXIS_DIM, 2, dtype=jnp.float32) / AXIS_DIM)
    ang = pos[:, None, :, None] * inv
    return jnp.cos(ang).astype(dtype), jnp.sin(ang).astype(dtype)


def axial_rope(x, cos, sin):
    b, l, h, _ = x.shape
    xa = x.reshape(b, l, h, 2, AXIS_DIM)
    x1, x2 = xa[..., :AXIS_DIM // 2], xa[..., AXIS_DIM // 2:]
    out = jnp.concatenate([x1 * cos - x2 * sin, x2 * cos + x1 * sin], axis=-1)
    return out.reshape(b, l, h, HEAD_DIM)


def split_qkv(p):
    b, l, _ = p.shape
    nq, nk = N_HEADS * HEAD_DIM, N_KV_HEADS * HEAD_DIM
    q = p[..., :nq].reshape(b, l, N_HEADS, HEAD_DIM)
    k = p[..., nq:nq + nk].reshape(b, l, N_KV_HEADS, HEAD_DIM)
    v = p[..., nq + nk:].reshape(b, l, N_KV_HEADS, HEAD_DIM)
    return q, k, v


def attend(q, k, v, mask, sink):
    b, lq = q.shape[:2]
    qg = q.reshape(b, lq, N_KV_HEADS, GROUP, HEAD_DIM)
    s = jnp.einsum('bqkgd,bskd->bkgqs', qg, k).astype(jnp.float32) * (HEAD_DIM ** -0.5)
    if mask is not None:
        s = jnp.where(mask, s, NEG_INF)
    m = jnp.max(s, axis=-1, keepdims=True)
    if sink is not None:
        sk = sink.astype(jnp.float32).reshape(1, N_KV_HEADS, GROUP, 1, 1)
        m = jnp.maximum(m, sk)
        p = jnp.exp(s - m)
        den = jnp.sum(p, axis=-1, keepdims=True) + jnp.exp(sk - m)
    else:
        p = jnp.exp(s - m)
        den = jnp.sum(p, axis=-1, keepdims=True)
    o = jnp.einsum('bkgqs,bskd->bqkgd', (p / den).astype(v.dtype), v)
    return o.reshape(b, lq, N_HEADS * HEAD_DIM)


def window_gqa(h_lat, h_ctx, w_qkv, b_qkv, sink, w_o, b_o, cos, sin, need_ctx):
    b, l, _ = h_lat.shape
    q, k, v = split_qkv(h_lat @ w_qkv + b_qkv)
    q, k = axial_rope(q, cos, sin), axial_rope(k, cos, sin)
    qc, kc, vc = split_qkv(h_ctx @ w_qkv + b_qkv)
    pad = ((0, 0), (WINDOW, WINDOW), (0, 0), (0, 0))
    kp, vp = jnp.pad(k, pad), jnp.pad(v, pad)
    nb = l // Q_BLOCK
    span = Q_BLOCK + 2 * WINDOW
    qb = q.reshape(b, nb, Q_BLOCK, N_HEADS, HEAD_DIM).swapaxes(0, 1)
    ctx_ok = jnp.ones((Q_BLOCK, kc.shape[1]), bool)

    def block(args):
        qblk, bi = args
        start = bi * Q_BLOCK
        kblk = lax.dynamic_slice_in_dim(kp, start, span, axis=1)
        vblk = lax.dynamic_slice_in_dim(vp, start, span, axis=1)
        qpos = start + jnp.arange(Q_BLOCK)
        kpos = start - WINDOW + jnp.arange(span)
        band = (jnp.abs(qpos[:, None] - kpos[None, :]) <= WINDOW) & (kpos >= 0)[None, :] & (kpos < l)[None, :]
        mask = jnp.concatenate([band, ctx_ok], axis=1)
        return attend(qblk, jnp.concatenate([kblk, kc], axis=1), jnp.concatenate([vblk, vc], axis=1), mask, sink)

    o = lax.map(block, (qb, jnp.arange(nb)))
    y_lat = o.swapaxes(0, 1).reshape(b, l, N_HEADS * HEAD_DIM) @ w_o + b_o
    y_ctx = attend(qc, kc, vc, None, sink) @ w_o + b_o if need_ctx else None
    return y_lat, y_ctx


def global_qknorm_gqa(h_lat, h_ctx, w_qkv, g_q, g_k, w_o, cos, sin, need_ctx):
    b, l, _ = h_lat.shape
    q, k, v = split_qkv(h_lat @ w_qkv)
    q = axial_rope(rms_norm(q, g_q), cos, sin)
    k = axial_rope(rms_norm(k, g_k), cos, sin)
    qc, kc, vc = split_qkv(h_ctx @ w_qkv)
    qc, kc = rms_norm(qc, g_q), rms_norm(kc, g_k)
    k_all = jnp.concatenate([k, kc], axis=1)
    v_all = jnp.concatenate([v, vc], axis=1)
    nb = l // Q_BLOCK
    qb = q.reshape(b, nb, Q_BLOCK, N_HEADS, HEAD_DIM).swapaxes(0, 1)
    o = lax.map(lambda qblk: attend(qblk, k_all, v_all, None, None), qb)
    y_lat = o.swapaxes(0, 1).reshape(b, l, N_HEADS * HEAD_DIM) @ w_o
    y_ctx = attend(qc, kc, vc, None, None) @ w_o if need_ctx else None
    return y_lat, y_ctx


def short_conv(x, w):
    return lax.conv_general_dilated(x, w[:, None, :].astype(x.dtype), window_strides=(1,),
                                    padding=[(CONV_K // 2, CONV_K // 2)],
                                    dimension_numbers=('NWC', 'WIO', 'NWC'),
                                    feature_group_count=x.shape[-1])


def gdn_project(h, w_in, conv_w, a_log, dt_bias):
    b, l, _ = h.shape
    p = h @ w_in
    qkv = jax.nn.silu(short_conv(p[..., :2 * GDN_KW + GDN_VW], conv_w)).astype(jnp.float32)
    q = l2_norm(qkv[..., :GDN_KW].reshape(b, l, GDN_HEADS, GDN_DK)) * (GDN_DK ** -0.5)
    k = l2_norm(qkv[..., GDN_KW:2 * GDN_KW].reshape(b, l, GDN_HEADS, GDN_DK))
    v = qkv[..., 2 * GDN_KW:].reshape(b, l, GDN_HEADS, GDN_DV)
    o = 2 * GDN_KW + GDN_VW
    z = p[..., o:o + GDN_VW].reshape(b, l, GDN_HEADS, GDN_DV)
    o += GDN_VW
    beta = jax.nn.sigmoid(p[..., o:o + 2 * GDN_HEADS].astype(jnp.float32)).reshape(b, l, 2, GDN_HEADS)
    a = p[..., o + 2 * GDN_HEADS:].astype(jnp.float32).reshape(b, l, 2, GDN_HEADS)
    g = -jnp.exp(a_log.astype(jnp.float32)) * jax.nn.softplus(a + dt_bias.astype(jnp.float32))
    return q, k, v, z, beta, g


def chunk_gated_delta(q, k, v, g, beta, s0):
    b, t, h, _ = q.shape
    dv = v.shape[-1]
    n = t // CHUNK

    def chunks(a):
        a = a.reshape(b, n, CHUNK, h, *a.shape[3:])
        return jnp.moveaxis(a, 3, 2).swapaxes(0, 1)

    qc, kc, vc, gc, bc = chunks(q), chunks(k), chunks(v), chunks(g), chunks(beta)
    gcum = jnp.cumsum(gc, axis=-1)
    lower = jnp.tril(jnp.ones((CHUNK, CHUNK), bool))
    strict = jnp.tril(jnp.ones((CHUNK, CHUNK), bool), -1)
    diff = gcum[..., :, None] - gcum[..., None, :]
    decay = jnp.where(lower, jnp.exp(jnp.where(lower, diff, 0.0)), 0.0)
    kb = kc * bc[..., None]
    a_mat = jnp.where(strict, jnp.einsum('nbhid,nbhjd->nbhij', kb, kc) * decay, 0.0)
    eye = jnp.eye(CHUNK, dtype=jnp.float32)
    t_inv = lax.linalg.triangular_solve(eye + a_mat, jnp.broadcast_to(eye, a_mat.shape),
                                        left_side=True, lower=True, unit_diagonal=True)
    u = t_inv @ (vc * bc[..., None])
    w = t_inv @ (kb * jnp.exp(gcum)[..., None])
    intra = jnp.einsum('nbhid,nbhjd->nbhij', qc, kc) * decay

    def step(s, xs):
        q_i, k_i, u_i, w_i, intra_i, g_i = xs
        v_new = u_i - w_i @ s
        o = (q_i * jnp.exp(g_i)[..., None]) @ s + intra_i @ v_new
        g_last = g_i[..., -1:]
        s = s * jnp.exp(g_last)[..., None] + jnp.einsum('bhcd,bhce->bhde', k_i * jnp.exp(g_last - g_i)[..., None], v_new)
        return s, o

    s, o = lax.scan(step, s0, (qc, kc, u, w, intra, gcum))
    o = jnp.moveaxis(o.swapaxes(0, 1), 2, 3).reshape(b, t, h, dv)
    return o, s


def gated_deltanet(h_lat, h_ctx, w_in, conv_w, a_log, dt_bias, g_out, w_o, need_ctx):
    ql, kl, vl, zl, bl, gl = gdn_project(h_lat, w_in, conv_w, a_log, dt_bias)
    qc, kc, vc, zc, bc, gc = gdn_project(h_ctx, w_in, conv_w, a_log, dt_bias)
    b = h_lat.shape[0]
    s0 = jnp.zeros((b, GDN_HEADS, GDN_DK, GDN_DV), jnp.float32)
    flip = lambda a: jnp.flip(a, axis=1)
    oc_f, sc_f = chunk_gated_delta(qc, kc, vc, gc[:, :, 0], bc[:, :, 0], s0)
    ol_f, _ = chunk_gated_delta(ql, kl, vl, gl[:, :, 0], bl[:, :, 0], sc_f)
    oc_b, sc_b = chunk_gated_delta(flip(qc), flip(kc), flip(vc), flip(gc[:, :, 1]), flip(bc[:, :, 1]), s0)
    ol_b, _ = chunk_gated_delta(flip(ql), flip(kl), flip(vl), flip(gl[:, :, 1]), flip(bl[:, :, 1]), sc_b)

    def readout(o, z):
        o = (rms_norm(o, g_out) * jax.nn.silu(z.astype(jnp.float32))).astype(h_lat.dtype)
        return o.reshape(b, o.shape[1], GDN_VW) @ w_o

    y_lat = readout(ol_f + flip(ol_b), zl)
    y_ctx = readout(oc_f + flip(oc_b), zc) if need_ctx else None
    return y_lat, y_ctx


def moe_ffn(h, w_router, b_router, w_up, b_up, w_down, b_down):
    n, d = h.shape
    logits = (h @ w_router + b_router).astype(jnp.float32)
    top_val, top_idx = lax.top_k(logits, TOP_K)
    gates = jax.nn.softmax(top_val, axis=-1).astype(h.dtype)
    flat_e = top_idx.reshape(-1)
    flat_tok = jnp.repeat(jnp.arange(n, dtype=jnp.int32), TOP_K)
    order = jnp.argsort(flat_e)
    e_sorted = flat_e[order]
    counts = jnp.bincount(flat_e, length=N_EXPERTS)
    padded = (counts + EXPERT_BLOCK - 1) // EXPERT_BLOCK * EXPERT_BLOCK
    start = jnp.cumsum(counts) - counts
    pend = jnp.cumsum(padded)
    pstart = pend - padded
    dest = pstart[e_sorted] + jnp.arange(n * TOP_K) - start[e_sorted]
    n_blocks = -(-(n * TOP_K) // EXPERT_BLOCK) + N_EXPERTS
    rows = n_blocks * EXPERT_BLOCK
    tok_buf = jnp.full((rows,), n, jnp.int32).at[dest].set(flat_tok[order])
    gate_buf = jnp.zeros((rows,), h.dtype).at[dest].set(gates.reshape(-1)[order])
    blk_expert = jnp.minimum(jnp.searchsorted(pend, jnp.arange(n_blocks) * EXPERT_BLOCK, side='right'), N_EXPERTS - 1)
    xb = jnp.concatenate([h, jnp.zeros((1, d), h.dtype)], axis=0)[tok_buf].reshape(n_blocks, EXPERT_BLOCK, d)

    def expert_block(args):
        xe, e = args
        u = xe @ w_up[e] + b_up[e]
        glu = jnp.minimum(u[..., :D_EXPERT], SWIGLU_LIMIT)
        lin = jnp.clip(u[..., D_EXPERT:], -SWIGLU_LIMIT, SWIGLU_LIMIT)
        act = glu * jax.nn.sigmoid(SWIGLU_ALPHA * glu) * (lin + 1.0)
        return act @ w_down[e] + b_down[e]

    yb = lax.map(expert_block, (xb, blk_expert)).reshape(rows, d) * gate_buf[:, None]
    return jnp.zeros((n + 1, d), h.dtype).at[tok_buf].add(yb)[:n]


def setup_inputs(seed: int = 0) -> dict:
    key = jax.random.key(seed)
    ks = jax.random.split(key, 32)
    f32 = jnp.float32
    nrm = lambda k, shape, scale: jax.random.normal(k, shape, f32) * scale
    d = D_MODEL
    dt = jnp.exp(jax.random.uniform(ks[20], (N_GDN, 2, GDN_HEADS), f32, np.log(1e-3), np.log(1e-1)))
    return {
        'x': nrm(ks[0], (BATCH, SEQ, d), 1.0),
        'c': nrm(ks[1], (BATCH, d), 1.0),
        'ctx': nrm(ks[2], (BATCH, CTX_LEN, d), 1.0),
        'c_ctx': nrm(ks[3], (d,), 1.0),
        'w_mod': nrm(ks[4], (DEPTH, d, 6 * d), 0.5 * d ** -0.5),
        'b_mod': nrm(ks[5], (DEPTH, 6 * d), 0.02),
        'g_mix': 1.0 + nrm(ks[6], (DEPTH, d), 0.02),
        'g_ffn': 1.0 + nrm(ks[7], (DEPTH, d), 0.02),
        'win_w_qkv': nrm(ks[8], (N_WIN, d, QKV_DIM), d ** -0.5),
        'win_b_qkv': nrm(ks[9], (N_WIN, QKV_DIM), 0.02),
        'win_sink': nrm(ks[10], (N_WIN, N_HEADS), 1.0),
        'win_w_o': nrm(ks[11], (N_WIN, N_HEADS * HEAD_DIM, d), (N_HEADS * HEAD_DIM) ** -0.5),
        'win_b_o': nrm(ks[12], (N_WIN, d), 0.02),
        'glb_w_qkv': nrm(ks[13], (N_GLB, d, QKV_DIM), d ** -0.5),
        'glb_g_q': 1.0 + nrm(ks[14], (N_GLB, HEAD_DIM), 0.02),
        'glb_g_k': 1.0 + nrm(ks[15], (N_GLB, HEAD_DIM), 0.02),
        'glb_w_o': nrm(ks[16], (N_GLB, N_HEADS * HEAD_DIM, d), (N_HEADS * HEAD_DIM) ** -0.5),
        'gdn_w_in': nrm(ks[17], (N_GDN, d, GDN_IN_DIM), d ** -0.5),
        'gdn_conv_w': nrm(ks[18], (N_GDN, CONV_K, 2 * GDN_KW + GDN_VW), CONV_K ** -0.5),
        'gdn_a_log': jnp.log(jax.random.uniform(ks[19], (N_GDN, 2, GDN_HEADS), f32, 1.0, 16.0)),
        'gdn_dt_bias': jnp.log(jnp.expm1(dt)),
        'gdn_g_out': 1.0 + nrm(ks[21], (N_GDN, GDN_DV), 0.02),
        'gdn_w_o': nrm(ks[22], (N_GDN, GDN_VW, d), GDN_VW ** -0.5),
        'moe_w_router': nrm(ks[23], (DEPTH, d, N_EXPERTS), d ** -0.5),
        'moe_b_router': nrm(ks[24], (DEPTH, N_EXPERTS), 0.01),
        'moe_w_up': nrm(ks[25], (DEPTH, N_EXPERTS, d, 2 * D_EXPERT), d ** -0.5),
        'moe_b_up': nrm(ks[26], (DEPTH, N_EXPERTS, 2 * D_EXPERT), 0.02),
        'moe_w_down': nrm(ks[27], (DEPTH, N_EXPERTS, D_EXPERT, d), D_EXPERT ** -0.5),
        'moe_b_down': nrm(ks[28], (DEPTH, N_EXPERTS, d), 0.02),
        'g_final': 1.0 + nrm(ks[29], (d,), 0.02),
    }


def reference(x, c, ctx, c_ctx, w_mod, b_mod, g_mix, g_ffn, win_w_qkv, win_b_qkv, win_sink, win_w_o, win_b_o,
              glb_w_qkv, glb_g_q, glb_g_k, glb_w_o, gdn_w_in, gdn_conv_w, gdn_a_log, gdn_dt_bias, gdn_g_out, gdn_w_o,
              moe_w_router, moe_b_router, moe_w_up, moe_b_up, moe_w_down, moe_b_down, g_final):
    b, l, d = x.shape
    cos, sin = axial_tables(l, x.dtype)
    x_lat, x_ctx = x, ctx
    for i in range(DEPTH):
        kind, j = i % N_MIXERS, i // N_MIXERS
        need_ctx = i < DEPTH - 1
        sh1, sc1, gt1, sh2, sc2, gt2 = adaln(c, w_mod[i], b_mod[i])
        csh1, csc1, cgt1, csh2, csc2, cgt2 = adaln(c_ctx[None, :], w_mod[i], b_mod[i])
        h_lat = rms_norm(x_lat, g_mix[i]) * (1.0 + sc1) + sh1
        h_ctx = rms_norm(x_ctx, g_mix[i]) * (1.0 + csc1) + csh1
        if kind == 0:
            y_lat, y_ctx = window_gqa(h_lat, h_ctx, win_w_qkv[j], win_b_qkv[j], win_sink[j], win_w_o[j], win_b_o[j],
                                      cos, sin, need_ctx)
        elif kind == 1:
            y_lat, y_ctx = global_qknorm_gqa(h_lat, h_ctx, glb_w_qkv[j], glb_g_q[j], glb_g_k[j], glb_w_o[j],
                                             cos, sin, need_ctx)
        else:
            y_lat, y_ctx = gated_deltanet(h_lat, h_ctx, gdn_w_in[j], gdn_conv_w[j], gdn_a_log[j], gdn_dt_bias[j],
                                          gdn_g_out[j], gdn_w_o[j], need_ctx)
        x_lat = x_lat + gt1 * y_lat
        h_lat = rms_norm(x_lat, g_ffn[i]) * (1.0 + sc2) + sh2
        if need_ctx:
            x_ctx = x_ctx + cgt1 * y_ctx
            h_ctx = rms_norm(x_ctx, g_ffn[i]) * (1.0 + csc2) + csh2
            n_ctx = h_ctx.shape[0] * h_ctx.shape[1]
            tokens = jnp.concatenate([h_ctx.reshape(-1, d), h_lat.reshape(-1, d)], axis=0)
            y = moe_ffn(tokens, moe_w_router[i], moe_b_router[i], moe_w_up[i], moe_b_up[i], moe_w_down[i], moe_b_down[i])
            x_ctx = x_ctx + cgt2 * y[:n_ctx].reshape(x_ctx.shape)
            x_lat = x_lat + gt2 * y[n_ctx:].reshape(x_lat.shape)
        else:
            y = moe_ffn(h_lat.reshape(-1, d), moe_w_router[i], moe_b_router[i], moe_w_up[i], moe_b_up[i],
                        moe_w_down[i], moe_b_down[i])
            x_lat = x_lat + gt2 * y.reshape(x_lat.shape)
    return rms_norm(x_lat, g_final)
```

```python
import functools

import jax
import jax.numpy as jnp
from jax import lax
from jax.experimental import pallas as pl
from jax.experimental.pallas import tpu as pltpu

F32 = jnp.float32
BF16 = jnp.bfloat16
HIGHEST = lax.Precision.HIGHEST

GRID_W = 64
HEAD_DIM = 64
N_HEADS = 16
N_KV_HEADS = 4
WINDOW = 128
ROPE_THETA = 10000.0
AXIS_DIM = HEAD_DIM // 2
GDN_HEADS = 8
GDN_DK = 128
CONV_K = 5
CHUNK = 64
N_EXPERTS = 32
TOP_K = 4
SWIGLU_LIMIT = 7.0
SWIGLU_ALPHA = 1.702
NORM_EPS = 1e-6
NEG_INF = -1e30

LANES = 128
ROW_TILE = 256
ATTN_TQ = 128
MOE_TILE = 256
VMEM_LIMIT = 48 * 1024 * 1024


def _params(sem):
    return pltpu.CompilerParams(dimension_semantics=sem, vmem_limit_bytes=VMEM_LIMIT)


def _rms(x, eps=NORM_EPS):
    return x * lax.rsqrt(jnp.mean(x * x, axis=-1, keepdims=True) + eps)


def _dot(a, b):
    return jnp.dot(a, b, preferred_element_type=F32)


def _dot_nt(a, b):
    return lax.dot_general(a, b, (((1,), (1,)), ((), ())), preferred_element_type=F32)


def _dot_tn(a, b):
    return lax.dot_general(a, b, (((0,), (0,)), ((), ())), preferred_element_type=F32)


def _adaln_kernel(cond_ref, w_ref, b_ref, o_ref):
    c = cond_ref[...]
    a = (c * jax.nn.sigmoid(c)).astype(BF16)
    o_ref[0] = _dot(a, w_ref[0].astype(BF16)) + b_ref[0]


def _adaln(cond, w_mod, b_mod):
    depth, d, n = w_mod.shape
    r = cond.shape[0]
    tn = n // 4
    return pl.pallas_call(
        _adaln_kernel,
        grid=(depth, n // tn),
        in_specs=[
            pl.BlockSpec((r, d), lambda i, j: (0, 0)),
            pl.BlockSpec((1, d, tn), lambda i, j: (i, 0, j)),
            pl.BlockSpec((1, 1, tn), lambda i, j: (i, 0, j)),
        ],
        out_specs=pl.BlockSpec((1, r, tn), lambda i, j: (i, 0, j)),
        out_shape=jax.ShapeDtypeStruct((depth, r, n), F32),
        compiler_params=_params(("parallel", "parallel")),
        name="adaln",
    )(cond, w_mod, b_mod.reshape(depth, 1, n))


def _norm_mod(x, m, g, which):
    return _rms(x) * g * (1.0 + m[which + 1:which + 2]) + m[which:which + 1]


def _swap16(t):
    lane = lax.broadcasted_iota(jnp.int32, t.shape, 1)
    first = (lane // (AXIS_DIM // 2)) % 2 == 0
    return jnp.where(first, pltpu.roll(t, LANES - AXIS_DIM // 2, 1), pltpu.roll(t, AXIS_DIM // 2, 1))


def _head_sum_matrix():
    r = lax.broadcasted_iota(jnp.int32, (LANES, LANES), 0) // HEAD_DIM
    c = lax.broadcasted_iota(jnp.int32, (LANES, LANES), 1) // HEAD_DIM
    return (r == c).astype(F32)


def _qkv_kernel(x_ref, mod_ref, g_ref, w_ref, b_ref, cos_ref, sin_ref, hg_ref, o_ref, *, qk_norm):
    nq = N_HEADS * HEAD_DIM
    nk = N_KV_HEADS * HEAD_DIM
    h = _norm_mod(x_ref[0], mod_ref[0, 0], g_ref[...], 0)
    p = _dot(h.astype(BF16), w_ref[...]) + b_ref[...]
    cos = cos_ref[...]
    sin = sin_ref[...]
    lane = lax.broadcasted_iota(jnp.int32, cos.shape, 1)
    low = lane < HEAD_DIM
    if qk_norm:
        seg = _head_sum_matrix()

    def expand(t, base):
        r = pltpu.roll(t, HEAD_DIM, 1)
        zero = jnp.zeros_like(t)
        parts = (jnp.where(low, t, zero), jnp.where(low, zero, r),
                 jnp.where(low, r, zero), jnp.where(low, zero, t))
        for i, part in enumerate(parts):
            o_ref[0, :, base + i * LANES: base + (i + 1) * LANES] = part.astype(o_ref.dtype)

    for j in range((nq + nk) // LANES):
        t = p[:, j * LANES:(j + 1) * LANES]
        if qk_norm:
            ss = jnp.dot(t * t, seg, precision=HIGHEST, preferred_element_type=F32)
            t = t * lax.rsqrt(ss * (1.0 / HEAD_DIM) + NORM_EPS) * hg_ref[:, j * LANES:(j + 1) * LANES]
        t = t * cos + _swap16(t) * sin
        if j < nq // LANES:
            o_ref[0, :, j * LANES:(j + 1) * LANES] = (t * (HEAD_DIM ** -0.5)).astype(o_ref.dtype)
        else:
            expand(t, nq + (j - nq // LANES) * 4 * LANES)
    for j in range(nk // LANES):
        t = p[:, nq + nk + j * LANES: nq + nk + (j + 1) * LANES]
        expand(t, nq + 4 * nk + j * 4 * LANES)


def _qkv_project(x, mod, g, w, b, cos, sin, hg, *, n_ctx, qk_norm):
    bsz, lt, d = x.shape
    n = w.shape[1]
    tm = ROW_TILE
    nct = n_ctx // tm
    n_out = N_HEADS * HEAD_DIM + 8 * N_KV_HEADS * HEAD_DIM
    return pl.pallas_call(
        functools.partial(_qkv_kernel, qk_norm=qk_norm),
        grid=(bsz, lt // tm),
        in_specs=[
            pl.BlockSpec((1, tm, d), lambda i, t: (i, t, 0)),
            pl.BlockSpec((1, 1, 6, d), lambda i, t: (i, (t >= nct).astype(jnp.int32), 0, 0)),
            pl.BlockSpec((1, d), lambda i, t: (0, 0)),
            pl.BlockSpec((d, n), lambda i, t: (0, 0)),
            pl.BlockSpec((1, n), lambda i, t: (0, 0)),
            pl.BlockSpec((tm, LANES), lambda i, t: (t, 0)),
            pl.BlockSpec((tm, LANES), lambda i, t: (t, 0)),
            pl.BlockSpec((1, hg.shape[1]), lambda i, t: (0, 0)),
        ],
        out_specs=pl.BlockSpec((1, tm, n_out), lambda i, t: (i, t, 0)),
        out_shape=jax.ShapeDtypeStruct((bsz, lt, n_out), BF16),
        compiler_params=_params(("parallel", "parallel")),
        name="qkv_project",
    )(x, mod, g, w, b, cos, sin, hg)


def _gdn_in_kernel(x_ref, mod_ref, g_ref, w_ref, o_ref, o2_ref):
    h = _norm_mod(x_ref[0], mod_ref[0, 0], g_ref[...], 0)
    p = _dot(h.astype(BF16), w_ref[...])
    n_main = o_ref.shape[2]
    o_ref[0] = p[:, :n_main].astype(o_ref.dtype)
    o2_ref[0] = p[:, n_main:]


def _gdn_in_project(x, mod, g, w, *, n_ctx, n_main):
    bsz, lt, d = x.shape
    n = w.shape[1]
    tm = ROW_TILE
    nct = n_ctx // tm
    return pl.pallas_call(
        _gdn_in_kernel,
        grid=(bsz, lt // tm),
        in_specs=[
            pl.BlockSpec((1, tm, d), lambda i, t: (i, t, 0)),
            pl.BlockSpec((1, 1, 6, d), lambda i, t: (i, (t >= nct).astype(jnp.int32), 0, 0)),
            pl.BlockSpec((1, d), lambda i, t: (0, 0)),
            pl.BlockSpec((d, n), lambda i, t: (0, 0)),
        ],
        out_specs=[
            pl.BlockSpec((1, tm, n_main), lambda i, t: (i, t, 0)),
            pl.BlockSpec((1, tm, n - n_main), lambda i, t: (i, t, 0)),
        ],
        out_shape=[
            jax.ShapeDtypeStruct((bsz, lt, n_main), BF16),
            jax.ShapeDtypeStruct((bsz, lt, n - n_main), F32),
        ],
        compiler_params=_params(("parallel", "parallel")),
        name="gdn_in_project",
    )(x, mod, g, w)


def _attn_kernel(sink_ref, q_ref, k_ref, v_ref, o_ref, *, windowed, n_ctx, seq):
    tq = q_ref.shape[1]
    g = pl.program_id(1)
    t = pl.program_id(2)
    nct = n_ctx // tq
    qs = jnp.concatenate([q_ref[0, :, 0:LANES], q_ref[0, :, LANES:2 * LANES]], axis=0)
    top = lax.broadcasted_iota(jnp.int32, (2 * tq, 1), 0) < tq

    def run(segments):
        acc = jnp.zeros((2 * tq, LANES), F32)
        for half in range(2):
            cols = slice(half * LANES, (half + 1) * LANES)
            scores = []
            for rows, mask in segments:
                s = _dot_nt(qs, k_ref[0, rows, cols])
                if mask is not None:
                    s = jnp.where(mask, s, NEG_INF)
                scores.append(s)
            m = scores[0].max(axis=-1, keepdims=True)
            for s in scores[1:]:
                m = jnp.maximum(m, s.max(axis=-1, keepdims=True))
            if windowed:
                sink = jnp.where(top, sink_ref[g * 4 + half], sink_ref[g * 4 + 2 + half])
                m = jnp.maximum(m, sink)
                den = jnp.exp(sink - m)
            else:
                den = jnp.zeros_like(m)
            o = jnp.zeros((2 * tq, LANES), F32)
            for s, (rows, _) in zip(scores, segments):
                p = jnp.exp(s - m)
                den = den + p.sum(axis=-1, keepdims=True)
                o = o + _dot(p.astype(BF16), v_ref[0, rows, cols])
            acc = acc + o * (1.0 / den)
        o_ref[0, :, 0:LANES] = acc[:tq].astype(o_ref.dtype)
        o_ref[0, :, LANES:2 * LANES] = acc[tq:].astype(o_ref.dtype)

    ctx_rows = slice(0, n_ctx)

    @pl.when(t < nct)
    def _():
        run([(ctx_rows, None)])

    @pl.when(t >= nct)
    def _():
        if windowed:
            span = tq + 2 * WINDOW
            start = (t - nct) * tq
            r0 = pl.multiple_of(jnp.minimum(n_ctx + start - WINDOW, n_ctx + seq - span), LANES)
            qpos = start + lax.broadcasted_iota(jnp.int32, (tq, span), 0)
            kpos = r0 - n_ctx + lax.broadcasted_iota(jnp.int32, (tq, span), 1)
            band = (jnp.abs(qpos - kpos) <= WINDOW) & (kpos >= 0)
            band = jnp.concatenate([band, band], axis=0)
            run([(pl.ds(r0, span), band), (ctx_rows, None)])
        else:
            run([(slice(0, n_ctx + seq), None)])


def _attention(qkv, sink, *, windowed, n_ctx):
    bsz, lt, _ = qkv.shape
    tq = ATTN_TQ
    gw = 2 * LANES
    nq = N_HEADS * HEAD_DIM
    kernel = functools.partial(_attn_kernel, windowed=windowed, n_ctx=n_ctx, seq=lt - n_ctx)
    return pl.pallas_call(
        kernel,
        grid=(bsz, N_KV_HEADS, lt // tq),
        in_specs=[
            pl.BlockSpec(memory_space=pltpu.SMEM),
            pl.BlockSpec((1, tq, gw), lambda i, g, t: (i, t, g)),
            pl.BlockSpec((1, lt, gw), lambda i, g, t: (i, 0, nq // gw + g)),
            pl.BlockSpec((1, lt, gw), lambda i, g, t: (i, 0, nq // gw + N_KV_HEADS + g)),
        ],
        out_specs=pl.BlockSpec((1, tq, gw), lambda i, g, t: (i, t, g)),
        out_shape=jax.ShapeDtypeStruct((bsz, lt, nq), BF16),
        compiler_params=_params(("parallel", "parallel", "arbitrary")),
        name="window_attention" if windowed else "global_attention",
    )(sink, qkv, qkv, qkv)


def _route_top4(h, wr_ref, br_ref, idx_ref, gate_ref):
    logits = jnp.dot(h, wr_ref[...], precision=HIGHEST, preferred_element_type=F32) + br_ref[...]
    lane = lax.broadcasted_iota(jnp.int32, logits.shape, 1)
    vals, idxs = [], []
    for _ in range(TOP_K):
        mx = logits.max(axis=-1, keepdims=True)
        ix = jnp.where(logits == mx, lane, LANES).min(axis=-1, keepdims=True)
        vals.append(mx)
        idxs.append(ix)
        logits = jnp.where(lane == ix, -jnp.inf, logits)
    es = [jnp.exp(v - vals[0]) for v in vals]
    inv = 1.0 / (es[0] + es[1] + es[2] + es[3])
    gates = jnp.zeros(logits.shape, F32)
    idx = jnp.zeros(logits.shape, jnp.int32)
    for k in range(TOP_K):
        gates = jnp.where(lane == k, es[k] * inv, gates)
        idx = jnp.where(lane == k, idxs[k], idx)
    idx_ref[0] = idx
    gate_ref[0] = gates


def _mix_out_tail(y, x_ref, m, gffn_ref, wr_ref, br_ref, xmid_ref, h2_ref, idx_ref, gate_ref):
    xm = x_ref[0] + m[2:3] * y
    xmid_ref[0] = xm
    h = _norm_mod(xm, m, gffn_ref[...], 3)
    h2_ref[0] = h.astype(h2_ref.dtype)
    _route_top4(h, wr_ref, br_ref, idx_ref, gate_ref)


def _attn_out_kernel(o_ref, x_ref, mod_ref, wo_ref, bo_ref, gffn_ref, wr_ref, br_ref,
                     xmid_ref, h2_ref, idx_ref, gate_ref):
    y = _dot(o_ref[0], wo_ref[...]) + bo_ref[...]
    _mix_out_tail(y, x_ref, mod_ref[0, 0], gffn_ref, wr_ref, br_ref, xmid_ref, h2_ref, idx_ref, gate_ref)


def _gdn_out_kernel(of_ref, ob_ref, z_ref, gout_ref, x_ref, mod_ref, wo_ref, bo_ref, gffn_ref, wr_ref, br_ref,
                    xmid_ref, h2_ref, idx_ref, gate_ref):
    gout = gout_ref[...]
    parts = []
    for hd in range(GDN_HEADS):
        cols = slice(hd * LANES, (hd + 1) * LANES)
        z = z_ref[0, :, cols].astype(F32)
        o = _rms(of_ref[0, :, cols] + ob_ref[0, :, cols]) * gout
        parts.append((o * (z * jax.nn.sigmoid(z))).astype(BF16))
    y = _dot(jnp.concatenate(parts, axis=1), wo_ref[...]) + bo_ref[...]
    _mix_out_tail(y, x_ref, mod_ref[0, 0], gffn_ref, wr_ref, br_ref, xmid_ref, h2_ref, idx_ref, gate_ref)


def _mixer_out(o_inputs, x, mod, w_o, b_o, g_ffn, w_r, b_r, *, n_ctx, gdn_gain=None):
    bsz, lt, d = x.shape
    tm = ROW_TILE
    nct = n_ctx // tm
    row = lambda i, t: (i, t, 0)
    const = lambda i, t: (0, 0)
    if gdn_gain is None:
        kernel = _attn_out_kernel
        head_specs = [pl.BlockSpec((1, tm, d), row)]
        head_args = list(o_inputs)
    else:
        kernel = _gdn_out_kernel
        o_f, o_b, p_main = o_inputs
        zblk = p_main.shape[2] // d - 1
        head_specs = [pl.BlockSpec((1, tm, d), row), pl.BlockSpec((1, tm, d), row),
                      pl.BlockSpec((1, tm, d), lambda i, t: (i, t, zblk)),
                      pl.BlockSpec((1, LANES), const)]
        head_args = [o_f, o_b, p_main, gdn_gain]
    return pl.pallas_call(
        kernel,
        grid=(bsz, lt // tm),
        in_specs=head_specs + [
            pl.BlockSpec((1, tm, d), row),
            pl.BlockSpec((1, 1, 6, d), lambda i, t: (i, (t >= nct).astype(jnp.int32), 0, 0)),
            pl.BlockSpec((d, d), const),
            pl.BlockSpec((1, d), const),
            pl.BlockSpec((1, d), const),
            pl.BlockSpec((d, LANES), const),
            pl.BlockSpec((1, LANES), const),
        ],
        out_specs=[
            pl.BlockSpec((1, tm, d), row),
            pl.BlockSpec((1, tm, d), row),
            pl.BlockSpec((1, tm, LANES), row),
            pl.BlockSpec((1, tm, LANES), row),
        ],
        out_shape=[
            jax.ShapeDtypeStruct((bsz, lt, d), F32),
            jax.ShapeDtypeStruct((bsz, lt, d), BF16),
            jax.ShapeDtypeStruct((bsz, lt, LANES), jnp.int32),
            jax.ShapeDtypeStruct((bsz, lt, LANES), F32),
        ],
        compiler_params=_params(("parallel", "parallel")),
        name="mixer_out_router",
    )(*head_args, x, mod, w_o, b_o, g_ffn, w_r, b_r)


def _moe_kernel(te_ref, nu_ref, x_ref, wu_ref, bu_ref, wd_ref, bd_ref, o_ref):
    i = pl.program_id(0)
    de = wd_ref.shape[1]

    @pl.when(i < nu_ref[0])
    def _():
        u = _dot(x_ref[...], wu_ref[0]) + bu_ref[0]
        glu = jnp.minimum(u[:, :de], SWIGLU_LIMIT)
        lin = jnp.clip(u[:, de:], -SWIGLU_LIMIT, SWIGLU_LIMIT)
        act = glu * jax.nn.sigmoid(SWIGLU_ALPHA * glu) * (lin + 1.0)
        o_ref[...] = (_dot(act.astype(BF16), wd_ref[0]) + bd_ref[0]).astype(o_ref.dtype)

    @pl.when(i >= nu_ref[0])
    def _():
        o_ref[...] = jnp.zeros_like(o_ref)


def _moe_experts(xs, tile_expert, n_used, w_up, b_up, w_down, b_down):
    rows, d = xs.shape
    tm = MOE_TILE
    n_tiles = rows // tm
    ne, _, n_up = w_up.shape
    de = w_down.shape[1]
    grid_spec = pltpu.PrefetchScalarGridSpec(
        num_scalar_prefetch=2,
        grid=(n_tiles,),
        in_specs=[
            pl.BlockSpec((tm, d), lambda i, te, nu: (jnp.minimum(i, nu[0] - 1), 0)),
            pl.BlockSpec((1, d, n_up), lambda i, te, nu: (te[i], 0, 0)),
            pl.BlockSpec((1, 1, n_up), lambda i, te, nu: (te[i], 0, 0)),
            pl.BlockSpec((1, de, d), lambda i, te, nu: (te[i], 0, 0)),
            pl.BlockSpec((1, 1, d), lambda i, te, nu: (te[i], 0, 0)),
        ],
        out_specs=pl.BlockSpec((tm, d), lambda i, te, nu: (i, 0)),
    )
    return pl.pallas_call(
        _moe_kernel,
        grid_spec=grid_spec,
        out_shape=jax.ShapeDtypeStruct((rows, d), BF16),
        compiler_params=_params(("arbitrary",)),
        name="moe_experts",
    )(tile_expert, n_used, xs, w_up, b_up.reshape(ne, 1, n_up), w_down, b_down.reshape(ne, 1, d))


def _combine_kernel(x_ref, yg_ref, gate_ref, mod_ref, o_ref):
    d = x_ref.shape[2]
    gates = gate_ref[0]
    acc = gates[:, 0:1] * yg_ref[:, 0:d].astype(F32)
    for k in range(1, TOP_K):
        acc = acc + gates[:, k:k + 1] * yg_ref[:, k * d:(k + 1) * d].astype(F32)
    o_ref[0] = x_ref[0] + mod_ref[0, 0][5:6] * acc


def _moe_combine(x, yg, gates, mod, *, n_ctx):
    bsz, lt, d = x.shape
    tm = ROW_TILE
    nct = n_ctx // tm
    nt = lt // tm
    row = lambda i, t: (i, t, 0)
    return pl.pallas_call(
        _combine_kernel,
        grid=(bsz, nt),
        in_specs=[
            pl.BlockSpec((1, tm, d), row),
            pl.BlockSpec((tm, TOP_K * d), lambda i, t: (i * nt + t, 0)),
            pl.BlockSpec((1, tm, LANES), row),
            pl.BlockSpec((1, 1, 6, d), lambda i, t: (i, (t >= nct).astype(jnp.int32), 0, 0)),
        ],
        out_specs=pl.BlockSpec((1, tm, d), row),
        out_shape=jax.ShapeDtypeStruct((bsz, lt, d), F32),
        compiler_params=_params(("parallel", "parallel")),
        name="moe_combine",
    )(x, yg, gates, mod)


def _moe_layer(x_mid, h2, idx, gates, mod, w_up, b_up, w_down, b_down, *, n_ctx):
    bsz, lt, d = x_mid.shape
    n = bsz * lt
    tm = MOE_TILE
    n_tiles = -(-(n * TOP_K) // tm) + N_EXPERTS
    flat_e = idx[..., :TOP_K].reshape(-1)
    onehot = (flat_e[:, None] == jnp.arange(N_EXPERTS, dtype=jnp.int32)[None, :]).astype(jnp.int32)
    csum = jnp.cumsum(onehot, axis=0)
    rank = jnp.take_along_axis(csum, flat_e[:, None], axis=1)[:, 0] - 1
    counts = csum[-1]
    padded = (counts + tm - 1) // tm * tm
    pend = jnp.cumsum(padded)
    dest = (pend - padded)[flat_e] + rank
    tok_buf = jnp.zeros((n_tiles * tm,), jnp.int32).at[dest].set(jnp.arange(n * TOP_K, dtype=jnp.int32) // TOP_K)
    n_used = (pend[-1] // tm).astype(jnp.int32)
    tile_e = jnp.searchsorted(pend, jnp.arange(n_tiles, dtype=jnp.int32) * tm, side='right').astype(jnp.int32)
    last_e = jnp.take(tile_e, jnp.maximum(n_used - 1, 0))
    tile_e = jnp.where(jnp.arange(n_tiles) < n_used, jnp.minimum(tile_e, N_EXPERTS - 1), last_e)
    xs = jnp.take(h2.reshape(n, d), tok_buf, axis=0)
    ys = _moe_experts(xs, tile_e, n_used.reshape(1), w_up, b_up, w_down, b_down)
    yg = jnp.take(ys, dest, axis=0).reshape(n, TOP_K * d)
    return _moe_combine(x_mid, yg, gates, mod, n_ctx=n_ctx)


def _gdn_conv_kernel(p_ref, w_ref, o_ref, *, n_ctx):
    c = pl.program_id(1)
    lt, tc = p_ref.shape[1], p_ref.shape[2]
    x = p_ref[0].astype(F32)
    row = lax.broadcasted_iota(jnp.int32, (lt, 1), 0)
    is_ctx = row < n_ctx
    pos = jnp.where(is_ctx, row, row - n_ctx)
    seg_len = jnp.where(is_ctx, n_ctx, lt - n_ctx)
    acc = x * w_ref[CONV_K // 2:CONV_K // 2 + 1, :]
    for j in range(CONV_K):
        dlt = j - CONV_K // 2
        if dlt == 0:
            continue
        shifted = pltpu.roll(x, (-dlt) % lt, 0)
        valid = (pos + dlt >= 0) & (pos + dlt < seg_len)
        acc = acc + jnp.where(valid, shifted, 0.0) * w_ref[j:j + 1, :]
    y = acc * jax.nn.sigmoid(acc)
    n_qk_tiles = 2 * GDN_HEADS * GDN_DK // tc
    is_qk = c < n_qk_tiles
    scale = jnp.where(c < n_qk_tiles // 2, GDN_DK ** -0.5, 1.0)
    for hd in range(tc // LANES):
        t = y[:, hd * LANES:(hd + 1) * LANES]
        normed = t * lax.rsqrt(jnp.sum(t * t, axis=-1, keepdims=True) + NORM_EPS) * scale
        o_ref[0, :, hd * LANES:(hd + 1) * LANES] = jnp.where(is_qk, normed, t).astype(o_ref.dtype)


def _gdn_conv(p_main, conv_w, *, n_ctx):
    bsz, lt, _ = p_main.shape
    n = conv_w.shape[1]
    tc = 4 * LANES
    return pl.pallas_call(
        functools.partial(_gdn_conv_kernel, n_ctx=n_ctx),
        grid=(bsz, n // tc),
        in_specs=[
            pl.BlockSpec((1, lt, tc), lambda i, c: (i, 0, c)),
            pl.BlockSpec((CONV_K, tc), lambda i, c: (0, c)),
        ],
        out_specs=pl.BlockSpec((1, lt, tc), lambda i, c: (i, 0, c)),
        out_shape=jax.ShapeDtypeStruct((bsz, lt, n), BF16),
        compiler_params=_params(("parallel", "parallel")),
        name="gdn_conv",
    )(p_main, conv_w)


def _unit_tri_inverse(a):
    n = a.shape[0]
    eye = (lax.broadcasted_iota(jnp.int32, (n, n), 0) == lax.broadcasted_iota(jnp.int32, (n, n), 1)).astype(F32)
    x = eye - a
    p = a
    steps = max(n - 1, 1).bit_length() - 1
    for _ in range(steps):
        pb = p.astype(BF16)
        p = _dot(pb, pb)
        x = x + _dot(x.astype(BF16), p.astype(BF16))
    return x


def _gdn_delta_kernel(qf_ref, kf_ref, vf_ref, qb_ref, kb_ref, vb_ref, colf_ref, colb_ref, rowf_ref, rowb_ref,
                      of_ref, ob_ref, s_ref):
    step = pl.program_id(1)

    @pl.when(step == 0)
    def _():
        s_ref[...] = jnp.zeros_like(s_ref)

    ii = lax.broadcasted_iota(jnp.int32, (CHUNK, CHUNK), 0)
    jj = lax.broadcasted_iota(jnp.int32, (CHUNK, CHUNK), 1)
    dirs = (
        (0, qf_ref, kf_ref, vf_ref, colf_ref, rowf_ref, of_ref, ii >= jj, ii > jj, CHUNK - 1),
        (1, qb_ref, kb_ref, vb_ref, colb_ref, rowb_ref, ob_ref, ii <= jj, ii < jj, 0),
    )
    for d, q_ref, k_ref, v_ref, col_ref, row_ref, o_ref, tri, strict, last in dirs:
        col = col_ref[0, 0]
        rows = row_ref[0, 0]
        for hd in range(GDN_HEADS):
            cols = slice(hd * LANES, (hd + 1) * LANES)
            q = q_ref[0, :, cols].astype(F32)
            k = k_ref[0, :, cols].astype(F32)
            v = v_ref[0, :, cols].astype(F32)
            gi = d * GDN_HEADS + hd
            gc = col[:, gi:gi + 1]
            beta = col[:, 2 * GDN_HEADS + gi:2 * GDN_HEADS + gi + 1]
            gr = rows[gi:gi + 1, :]
            decay = jnp.where(tri, jnp.exp(jnp.where(tri, gc - gr, 0.0)), 0.0)
            kbeta = k * beta
            kb16 = k.astype(BF16)
            a = jnp.where(strict, _dot_nt(kbeta.astype(BF16), kb16) * decay, 0.0)
            t_inv = _unit_tri_inverse(a).astype(BF16)
            u = _dot(t_inv, (v * beta).astype(BF16))
            w = _dot(t_inv, (kbeta * jnp.exp(gc)).astype(BF16))
            intra = _dot_nt(q.astype(BF16), kb16) * decay
            s = s_ref[gi]
            s16 = s.astype(BF16)
            v_new = u - _dot(w.astype(BF16), s16)
            o = _dot((q * jnp.exp(gc)).astype(BF16), s16) + _dot(intra.astype(BF16), v_new.astype(BF16))
            o_ref[0, :, cols] = o
            g_last = gc[last:last + 1, :]
            s_ref[gi] = s * jnp.exp(g_last) + _dot_tn((k * jnp.exp(g_last - gc)).astype(BF16), v_new.astype(BF16))


def _gdn_delta(qkv, col_pack, row_pack, *, n_ctx):
    bsz, lt, _ = qkv.shape
    nc = lt // CHUNK
    ncc = n_ctx // CHUNK
    kw = GDN_HEADS * GDN_DK

    def bwd_chunk(c):
        return jnp.where(c < ncc, ncc - 1 - c, nc - 1 - (c - ncc))

    fwd = lambda blk: pl.BlockSpec((1, CHUNK, kw), lambda i, c: (i, c, blk))
    bwd = lambda blk: pl.BlockSpec((1, CHUNK, kw), lambda i, c: (i, bwd_chunk(c), blk))
    ncol, nrow = col_pack.shape[3], row_pack.shape[2]
    return pl.pallas_call(
        _gdn_delta_kernel,
        grid=(bsz, nc),
        in_specs=[
            fwd(0), fwd(1), fwd(2), bwd(0), bwd(1), bwd(2),
            pl.BlockSpec((1, 1, CHUNK, ncol), lambda i, c: (i, c, 0, 0)),
            pl.BlockSpec((1, 1, CHUNK, ncol), lambda i, c: (i, bwd_chunk(c), 0, 0)),
            pl.BlockSpec((1, 1, nrow, CHUNK), lambda i, c: (i, c, 0, 0)),
            pl.BlockSpec((1, 1, nrow, CHUNK), lambda i, c: (i, bwd_chunk(c), 0, 0)),
        ],
        out_specs=[
            pl.BlockSpec((1, CHUNK, kw), lambda i, c: (i, c, 0)),
            pl.BlockSpec((1, CHUNK, kw), lambda i, c: (i, bwd_chunk(c), 0)),
        ],
        out_shape=[jax.ShapeDtypeStruct((bsz, lt, kw), F32), jax.ShapeDtypeStruct((bsz, lt, kw), F32)],
        scratch_shapes=[pltpu.VMEM((2 * GDN_HEADS, GDN_DK, GDN_DK), F32)],
        compiler_params=_params(("parallel", "arbitrary")),
        name="gdn_delta",
    )(qkv, qkv, qkv, qkv, qkv, qkv, col_pack, col_pack, row_pack, row_pack)


def _gdn_gates(p_small, a_log, dt_bias):
    bsz, lt, _ = p_small.shape
    nh = GDN_HEADS
    beta = jax.nn.sigmoid(p_small[..., :2 * nh])
    a = p_small[..., 2 * nh:4 * nh]
    g = -jnp.exp(a_log.reshape(2 * nh)) * jax.nn.softplus(a + dt_bias.reshape(2 * nh))
    gch = g.reshape(bsz, lt // CHUNK, CHUNK, 2 * nh)
    gf = jnp.cumsum(gch[..., :nh], axis=2)
    gb = jnp.flip(jnp.cumsum(jnp.flip(gch[..., nh:], axis=2), axis=2), axis=2)
    gc = jnp.concatenate([gf, gb], axis=-1)
    col_pack = jnp.concatenate([gc, beta.reshape(bsz, lt // CHUNK, CHUNK, 2 * nh)], axis=-1)
    row_pack = jnp.swapaxes(gc, 2, 3)
    return col_pack, row_pack


def _final_norm_kernel(x_ref, g_ref, o_ref):
    o_ref[0] = _rms(x_ref[0]) * g_ref[...]


def _final_norm(x, g, *, n_ctx):
    bsz, lt, d = x.shape
    tm = ROW_TILE
    nct = n_ctx // tm
    return pl.pallas_call(
        _final_norm_kernel,
        grid=(bsz, (lt - n_ctx) // tm),
        in_specs=[pl.BlockSpec((1, tm, d), lambda i, t: (i, t + nct, 0)), pl.BlockSpec((1, d), lambda i, t: (0, 0))],
        out_specs=pl.BlockSpec((1, tm, d), lambda i, t: (i, t, 0)),
        out_shape=jax.ShapeDtypeStruct((bsz, lt - n_ctx, d), F32),
        compiler_params=_params(("parallel", "parallel")),
        name="final_norm",
    )(x, g)


def _rope_tables(n_ctx, seq):
    half = AXIS_DIM // 2
    lane = jnp.arange(LANES)
    inv = ROPE_THETA ** (-jnp.arange(0, AXIS_DIM, 2, dtype=F32) / AXIS_DIM)
    freq = inv[lane % half]
    use_col = (lane % HEAD_DIM) // AXIS_DIM == 1
    sign = jnp.where((lane // half) % 2 == 0, -1.0, 1.0)
    tok = jnp.arange(seq)
    pos = jnp.where(use_col[None, :], (tok % GRID_W)[:, None], (tok // GRID_W)[:, None]).astype(F32)
    ang = pos * freq[None, :]
    cos = jnp.concatenate([jnp.ones((n_ctx, LANES), F32), jnp.cos(ang)], axis=0)
    sin = jnp.concatenate([jnp.zeros((n_ctx, LANES), F32), jnp.sin(ang) * sign[None, :]], axis=0)
    return cos, sin


def _pad_cols(a, n):
    return jnp.pad(a, [(0, 0)] * (a.ndim - 1) + [(0, n - a.shape[-1])])


def kernel(x, c, ctx, c_ctx, w_mod, b_mod, g_mix, g_ffn, win_w_qkv, win_b_qkv, win_sink, win_w_o, win_b_o, glb_w_qkv, glb_g_q, glb_g_k, glb_w_o, gdn_w_in, gdn_conv_w, gdn_a_log, gdn_dt_bias, gdn_g_out, gdn_w_o, moe_w_router, moe_b_router, moe_w_up, moe_b_up, moe_w_down, moe_b_down, g_final):
    bsz, seq, d = x.shape
    n_ctx = ctx.shape[1]
    depth = w_mod.shape[0]
    assert n_ctx % ROW_TILE == 0 and seq % ROW_TILE == 0 and n_ctx >= WINDOW and seq >= ATTN_TQ + 2 * WINDOW
    nq, nk = N_HEADS * HEAD_DIM, N_KV_HEADS * HEAD_DIM

    xs = jnp.concatenate([ctx, x], axis=1)
    cond_rows = -(-(bsz + 1) // 8) * 8
    cond = jnp.zeros((cond_rows, d), F32).at[:bsz].set(c).at[bsz].set(c_ctx)
    mods = _adaln(cond, w_mod, b_mod).reshape(depth, cond_rows, 6, d)
    cos, sin = _rope_tables(n_ctx, seq)
    zero_bias = jnp.zeros((1, d), F32)

    for i in range(depth):
        kind, j = i % 3, i // 3
        mod = jnp.stack([jnp.broadcast_to(mods[i, bsz], (bsz, 6, d)), mods[i, :bsz]], axis=1)
        g1 = g_mix[i].reshape(1, d)
        w_r = _pad_cols(moe_w_router[i], LANES)
        b_r = jnp.concatenate([moe_b_router[i], jnp.full((LANES - N_EXPERTS,), NEG_INF, F32)]).reshape(1, LANES)
        route_args = (g_ffn[i].reshape(1, d), w_r, b_r)
        if kind == 0:
            qkv = _qkv_project(xs, mod, g1, win_w_qkv[j].astype(BF16), win_b_qkv[j].reshape(1, -1), cos, sin,
                               jnp.ones((1, nq + nk), F32), n_ctx=n_ctx, qk_norm=False)
            o = _attention(qkv, win_sink[j], windowed=True, n_ctx=n_ctx)
            outs = _mixer_out((o,), xs, mod, win_w_o[j].astype(BF16), win_b_o[j].reshape(1, d), *route_args,
                              n_ctx=n_ctx)
        elif kind == 1:
            hg = jnp.concatenate([jnp.tile(glb_g_q[j], N_HEADS), jnp.tile(glb_g_k[j], N_KV_HEADS)]).reshape(1, -1)
            qkv = _qkv_project(xs, mod, g1, glb_w_qkv[j].astype(BF16), jnp.zeros((1, nq + 2 * nk), F32), cos, sin,
                               hg, n_ctx=n_ctx, qk_norm=True)
            o = _attention(qkv, jnp.zeros((N_HEADS,), F32), windowed=False, n_ctx=n_ctx)
            outs = _mixer_out((o,), xs, mod, glb_w_o[j].astype(BF16), zero_bias, *route_args, n_ctx=n_ctx)
        else:
            n_main = 2 * GDN_HEADS * GDN_DK + 2 * GDN_HEADS * LANES
            w_in = _pad_cols(gdn_w_in[j], n_main + LANES).astype(BF16)
            p_main, p_small = _gdn_in_project(xs, mod, g1, w_in, n_ctx=n_ctx, n_main=n_main)
            qkv = _gdn_conv(p_main, gdn_conv_w[j], n_ctx=n_ctx)
            col_pack, row_pack = _gdn_gates(p_small, gdn_a_log[j], gdn_dt_bias[j])
            o_f, o_b = _gdn_delta(qkv, col_pack, row_pack, n_ctx=n_ctx)
            outs = _mixer_out((o_f, o_b, p_main), xs, mod, gdn_w_o[j].astype(BF16), zero_bias, *route_args,
                              n_ctx=n_ctx, gdn_gain=gdn_g_out[j].reshape(1, LANES))
        x_mid, h2, idx, gates = outs
        xs = _moe_layer(x_mid, h2, idx, gates, mod, moe_w_up[i].astype(BF16), moe_b_up[i],
                        moe_w_down[i].astype(BF16), moe_b_down[i], n_ctx=n_ctx)
    return _final_norm(xs, g_final.reshape(1, d), n_ctx=n_ctx)
```

```python
import functools

import jax
import jax.numpy as jnp
from jax import lax
from jax.experimental import pallas as pl
from jax.experimental.pallas import tpu as pltpu

F32 = jnp.float32
BF16 = jnp.bfloat16
HIGHEST = lax.Precision.HIGHEST

GRID_W = 64
HEAD_DIM = 64
N_HEADS = 16
N_KV_HEADS = 4
WINDOW = 128
ROPE_THETA = 10000.0
AXIS_DIM = HEAD_DIM // 2
GDN_HEADS = 8
GDN_DK = 128
CONV_K = 5
CHUNK = 64
N_EXPERTS = 32
TOP_K = 4
SWIGLU_LIMIT = 7.0
SWIGLU_ALPHA = 1.702
NORM_EPS = 1e-6
NEG_INF = -1e30

LANES = 128
ROW_TILE = 256
ATTN_TQ = 128
MOE_TILE = 256
VMEM_LIMIT = 48 * 1024 * 1024


def _params(sem):
    return pltpu.CompilerParams(dimension_semantics=sem, vmem_limit_bytes=VMEM_LIMIT)


def _rms(x, eps=NORM_EPS):
    return x * lax.rsqrt(jnp.mean(x * x, axis=-1, keepdims=True) + eps)


def _dot(a, b):
    return jnp.dot(a, b, preferred_element_type=F32)


def _dot_nt(a, b):
    return lax.dot_general(a, b, (((1,), (1,)), ((), ())), preferred_element_type=F32)


def _dot_tn(a, b):
    return lax.dot_general(a, b, (((0,), (0,)), ((), ())), preferred_element_type=F32)


def _adaln_kernel(cond_ref, w_ref, b_ref, o_ref):
    c = cond_ref[...]
    a = (c * jax.nn.sigmoid(c)).astype(BF16)
    o_ref[0] = _dot(a, w_ref[0].astype(BF16)) + b_ref[0]


def _adaln(cond, w_mod, b_mod):
    depth, d, n = w_mod.shape
    r = cond.shape[0]
    tn = n // 4
    return pl.pallas_call(
        _adaln_kernel,
        grid=(depth, n // tn),
        in_specs=[
            pl.BlockSpec((r, d), lambda i, j: (0, 0)),
            pl.BlockSpec((1, d, tn), lambda i, j: (i, 0, j)),
            pl.BlockSpec((1, 1, tn), lambda i, j: (i, 0, j)),
        ],
        out_specs=pl.BlockSpec((1, r, tn), lambda i, j: (i, 0, j)),
        out_shape=jax.ShapeDtypeStruct((depth, r, n), F32),
        compiler_params=_params(("parallel", "parallel")),
        name="adaln",
    )(cond, w_mod, b_mod.reshape(depth, 1, n))


def _norm_mod(x, m, g, which):
    return _rms(x) * g * (1.0 + m[which + 1:which + 2]) + m[which:which + 1]


def _swap16(t):
    lane = lax.broadcasted_iota(jnp.int32, t.shape, 1)
    first = (lane // (AXIS_DIM // 2)) % 2 == 0
    return jnp.where(first, pltpu.roll(t, LANES - AXIS_DIM // 2, 1), pltpu.roll(t, AXIS_DIM // 2, 1))


def _head_sum_matrix():
    r = lax.broadcasted_iota(jnp.int32, (LANES, LANES), 0) // HEAD_DIM
    c = lax.broadcasted_iota(jnp.int32, (LANES, LANES), 1) // HEAD_DIM
    return (r == c).astype(F32)


def _qkv_kernel(x_ref, mod_ref, g_ref, w_ref, b_ref, cos_ref, sin_ref, hg_ref, o_ref, *, qk_norm):
    nq = N_HEADS * HEAD_DIM
    nk = N_KV_HEADS * HEAD_DIM
    h = _norm_mod(x_ref[0], mod_ref[0, 0], g_ref[...], 0)
    p = _dot(h.astype(BF16), w_ref[...]) + b_ref[...]
    cos = cos_ref[...]
    sin = sin_ref[...]
    lane = lax.broadcasted_iota(jnp.int32, cos.shape, 1)
    low = lane < HEAD_DIM
    if qk_norm:
        seg = _head_sum_matrix()

    def expand(t, base):
        r = pltpu.roll(t, HEAD_DIM, 1)
        zero = jnp.zeros_like(t)
        parts = (jnp.where(low, t, zero), jnp.where(low, zero, r),
                 jnp.where(low, r, zero), jnp.where(low, zero, t))
        for i, part in enumerate(parts):
            o_ref[0, :, base + i * LANES: base + (i + 1) * LANES] = part.astype(o_ref.dtype)

    for j in range((nq + nk) // LANES):
        t = p[:, j * LANES:(j + 1) * LANES]
        if qk_norm:
            ss = jnp.dot(t * t, seg, precision=HIGHEST, preferred_element_type=F32)
            t = t * lax.rsqrt(ss * (1.0 / HEAD_DIM) + NORM_EPS) * hg_ref[:, j * LANES:(j + 1) * LANES]
        t = t * cos + _swap16(t) * sin
        if j < nq // LANES:
            o_ref[0, :, j * LANES:(j + 1) * LANES] = (t * (HEAD_DIM ** -0.5)).astype(o_ref.dtype)
        else:
            expand(t, nq + (j - nq // LANES) * 4 * LANES)
    for j in range(nk // LANES):
        t = p[:, nq + nk + j * LANES: nq + nk + (j + 1) * LANES]
        expand(t, nq + 4 * nk + j * 4 * LANES)


def _qkv_project(x, mod, g, w, b, cos, sin, hg, *, n_ctx, qk_norm):
    bsz, lt, d = x.shape
    n = w.shape[1]
    tm = ROW_TILE
    nct = n_ctx // tm
    n_out = N_HEADS * HEAD_DIM + 8 * N_KV_HEADS * HEAD_DIM
    return pl.pallas_call(
        functools.partial(_qkv_kernel, qk_norm=qk_norm),
        grid=(bsz, lt // tm),
        in_specs=[
            pl.BlockSpec((1, tm, d), lambda i, t: (i, t, 0)),
            pl.BlockSpec((1, 1, 6, d), lambda i, t: (i, (t >= nct).astype(jnp.int32), 0, 0)),
            pl.BlockSpec((1, d), lambda i, t: (0, 0)),
            pl.BlockSpec((d, n), lambda i, t: (0, 0)),
            pl.BlockSpec((1, n), lambda i, t: (0, 0)),
            pl.BlockSpec((tm, LANES), lambda i, t: (t, 0)),
            pl.BlockSpec((tm, LANES), lambda i, t: (t, 0)),
            pl.BlockSpec((1, hg.shape[1]), lambda i, t: (0, 0)),
        ],
        out_specs=pl.BlockSpec((1, tm, n_out), lambda i, t: (i, t, 0)),
        out_shape=jax.ShapeDtypeStruct((bsz, lt, n_out), BF16),
        compiler_params=_params(("parallel", "parallel")),
        name="qkv_project",
    )(x, mod, g, w, b, cos, sin, hg)


def _gdn_in_kernel(x_ref, mod_ref, g_ref, w_ref, o_ref, o2_ref):
    h = _norm_mod(x_ref[0], mod_ref[0, 0], g_ref[...], 0)
    p = _dot(h.astype(BF16), w_ref[...])
    n_main = o_ref.shape[2]
    o_ref[0] = p[:, :n_main].astype(o_ref.dtype)
    o2_ref[0] = p[:, n_main:]


def _gdn_in_project(x, mod, g, w, *, n_ctx, n_main):
    bsz, lt, d = x.shape
    n = w.shape[1]
    tm = ROW_TILE
    nct = n_ctx // tm
    return pl.pallas_call(
        _gdn_in_kernel,
        grid=(bsz, lt // tm),
        in_specs=[
            pl.BlockSpec((1, tm, d), lambda i, t: (i, t, 0)),
            pl.BlockSpec((1, 1, 6, d), lambda i, t: (i, (t >= nct).astype(jnp.int32), 0, 0)),
            pl.BlockSpec((1, d), lambda i, t: (0, 0)),
            pl.BlockSpec((d, n), lambda i, t: (0, 0)),
        ],
        out_specs=[
            pl.BlockSpec((1, tm, n_main), lambda i, t: (i, t, 0)),
            pl.BlockSpec((1, tm, n - n_main), lambda i, t: (i, t, 0)),
        ],
        out_shape=[
            jax.ShapeDtypeStruct((bsz, lt, n_main), BF16),
            jax.ShapeDtypeStruct((bsz, lt, n - n_main), F32),
        ],
        compiler_params=_params(("parallel", "parallel")),
        name="gdn_in_project",
    )(x, mod, g, w)


def _attn_kernel(sink_ref, q_ref, k_ref, v_ref, o_ref, *, windowed, n_ctx, seq):
    tq = q_ref.shape[1]
    g = pl.program_id(1)
    t = pl.program_id(2)
    nct = n_ctx // tq
    qs = jnp.concatenate([q_ref[0, :, 0:LANES], q_ref[0, :, LANES:2 * LANES]], axis=0)
    top = lax.broadcasted_iota(jnp.int32, (2 * tq, 1), 0) < tq

    def run(segments):
        acc = jnp.zeros((2 * tq, LANES), F32)
        for half in range(2):
            cols = slice(half * LANES, (half + 1) * LANES)
            scores = []
            for rows, mask in segments:
                s = _dot_nt(qs, k_ref[0, rows, cols])
                if mask is not None:
                    s = jnp.where(mask, s, NEG_INF)
                scores.append(s)
            m = scores[0].max(axis=-1, keepdims=True)
            for s in scores[1:]:
                m = jnp.maximum(m, s.max(axis=-1, keepdims=True))
            if windowed:
                sink = jnp.where(top, sink_ref[g * 4 + half], sink_ref[g * 4 + 2 + half])
                m = jnp.maximum(m, sink)
                den = jnp.exp(sink - m)
            else:
                den = jnp.zeros_like(m)
            o = jnp.zeros((2 * tq, LANES), F32)
            for s, (rows, _) in zip(scores, segments):
                p = jnp.exp(s - m)
                den = den + p.sum(axis=-1, keepdims=True)
                o = o + _dot(p.astype(BF16), v_ref[0, rows, cols])
            acc = acc + o * (1.0 / den)
        o_ref[0, :, 0:LANES] = acc[:tq].astype(o_ref.dtype)
        o_ref[0, :, LANES:2 * LANES] = acc[tq:].astype(o_ref.dtype)

    ctx_rows = slice(0, n_ctx)

    @pl.when(t < nct)
    def _():
        run([(ctx_rows, None)])

    @pl.when(t >= nct)
    def _():
        if windowed:
            span = tq + 2 * WINDOW
            start = (t - nct) * tq
            r0 = pl.multiple_of(jnp.minimum(n_ctx + start - WINDOW, n_ctx + seq - span), LANES)
            qpos = start + lax.broadcasted_iota(jnp.int32, (tq, span), 0)
            kpos = r0 - n_ctx + lax.broadcasted_iota(jnp.int32, (tq, span), 1)
            band = (jnp.abs(qpos - kpos) <= WINDOW) & (kpos >= 0)
            band = jnp.concatenate([band, band], axis=0)
            run([(pl.ds(r0, span), band), (ctx_rows, None)])
        else:
            run([(slice(0, n_ctx + seq), None)])


def _attention(qkv, sink, *, windowed, n_ctx):
    bsz, lt, _ = qkv.shape
    tq = ATTN_TQ
    gw = 2 * LANES
    nq = N_HEADS * HEAD_DIM
    kernel = functools.partial(_attn_kernel, windowed=windowed, n_ctx=n_ctx, seq=lt - n_ctx)
    return pl.pallas_call(
        kernel,
        grid=(bsz, N_KV_HEADS, lt // tq),
        in_specs=[
            pl.BlockSpec(memory_space=pltpu.SMEM),
            pl.BlockSpec((1, tq, gw), lambda i, g, t: (i, t, g)),
            pl.BlockSpec((1, lt, gw), lambda i, g, t: (i, 0, nq // gw + g)),
            pl.BlockSpec((1, lt, gw), lambda i, g, t: (i, 0, nq // gw + N_KV_HEADS + g)),
        ],
        out_specs=pl.BlockSpec((1, tq, gw), lambda i, g, t: (i, t, g)),
        out_shape=jax.ShapeDtypeStruct((bsz, lt, nq), BF16),
        compiler_params=_params(("parallel", "parallel", "arbitrary")),
        name="window_attention" if windowed else "global_attention",
    )(sink, qkv, qkv, qkv)


def _route_top4(h, wr_ref, br_ref, idx_ref, gate_ref):
    logits = jnp.dot(h, wr_ref[...], precision=HIGHEST, preferred_element_type=F32) + br_ref[...]
    lane = lax.broadcasted_iota(jnp.int32, logits.shape, 1)
    vals, idxs = [], []
    for _ in range(TOP_K):
        mx = logits.max(axis=-1, keepdims=True)
        ix = jnp.where(logits == mx, lane, LANES).min(axis=-1, keepdims=True)
        vals.append(mx)
        idxs.append(ix)
        logits = jnp.where(lane == ix, -jnp.inf, logits)
    es = [jnp.exp(v - vals[0]) for v in vals]
    inv = 1.0 / (es[0] + es[1] + es[2] + es[3])
    gates = jnp.zeros(logits.shape, F32)
    idx = jnp.zeros(logits.shape, jnp.int32)
    for k in range(TOP_K):
        gates = jnp.where(lane == k, es[k] * inv, gates)
        idx = jnp.where(lane == k, idxs[k], idx)
    idx_ref[0] = idx
    gate_ref[0] = gates


def _mix_out_tail(y, x_ref, m, gffn_ref, wr_ref, br_ref, xmid_ref, h2_ref, idx_ref, gate_ref):
    xm = x_ref[0] + m[2:3] * y
    xmid_ref[0] = xm
    h = _norm_mod(xm, m, gffn_ref[...], 3)
    h2_ref[0] = h.astype(h2_ref.dtype)
    _route_top4(h, wr_ref, br_ref, idx_ref, gate_ref)


def _attn_out_kernel(o_ref, x_ref, mod_ref, wo_ref, bo_ref, gffn_ref, wr_ref, br_ref,
                     xmid_ref, h2_ref, idx_ref, gate_ref):
    y = _dot(o_ref[0], wo_ref[...]) + bo_ref[...]
    _mix_out_tail(y, x_ref, mod_ref[0, 0], gffn_ref, wr_ref, br_ref, xmid_ref, h2_ref, idx_ref, gate_ref)


def _gdn_out_kernel(of_ref, ob_ref, z_ref, gout_ref, x_ref, mod_ref, wo_ref, bo_ref, gffn_ref, wr_ref, br_ref,
                    xmid_ref, h2_ref, idx_ref, gate_ref):
    gout = gout_ref[...]
    parts = []
    for hd in range(GDN_HEADS):
        cols = slice(hd * LANES, (hd + 1) * LANES)
        z = z_ref[0, :, cols].astype(F32)
        o = _rms(of_ref[0, :, cols] + ob_ref[0, :, cols]) * gout
        parts.append((o * (z * jax.nn.sigmoid(z))).astype(BF16))
    y = _dot(jnp.concatenate(parts, axis=1), wo_ref[...]) + bo_ref[...]
    _mix_out_tail(y, x_ref, mod_ref[0, 0], gffn_ref, wr_ref, br_ref, xmid_ref, h2_ref, idx_ref, gate_ref)


def _mixer_out(o_inputs, x, mod, w_o, b_o, g_ffn, w_r, b_r, *, n_ctx, gdn_gain=None):
    bsz, lt, d = x.shape
    tm = ROW_TILE
    nct = n_ctx // tm
    row = lambda i, t: (i, t, 0)
    const = lambda i, t: (0, 0)
    if gdn_gain is None:
        kernel = _attn_out_kernel
        head_specs = [pl.BlockSpec((1, tm, d), row)]
        head_args = list(o_inputs)
    else:
        kernel = _gdn_out_kernel
        o_f, o_b, p_main = o_inputs
        zblk = p_main.shape[2] // d - 1
        head_specs = [pl.BlockSpec((1, tm, d), row), pl.BlockSpec((1, tm, d), row),
                      pl.BlockSpec((1, tm, d), lambda i, t: (i, t, zblk)),
                      pl.BlockSpec((1, LANES), const)]
        head_args = [o_f, o_b, p_main, gdn_gain]
    return pl.pallas_call(
        kernel,
        grid=(bsz, lt // tm),
        in_specs=head_specs + [
            pl.BlockSpec((1, tm, d), row),
            pl.BlockSpec((1, 1, 6, d), lambda i, t: (i, (t >= nct).astype(jnp.int32), 0, 0)),
            pl.BlockSpec((d, d), const),
            pl.BlockSpec((1, d), const),
            pl.BlockSpec((1, d), const),
            pl.BlockSpec((d, LANES), const),
            pl.BlockSpec((1, LANES), const),
        ],
        out_specs=[
            pl.BlockSpec((1, tm, d), row),
            pl.BlockSpec((1, tm, d), row),
            pl.BlockSpec((1, tm, LANES), row),
            pl.BlockSpec((1, tm, LANES), row),
        ],
        out_shape=[
            jax.ShapeDtypeStruct((bsz, lt, d), F32),
            jax.ShapeDtypeStruct((bsz, lt, d), F32),
            jax.ShapeDtypeStruct((bsz, lt, LANES), jnp.int32),
            jax.ShapeDtypeStruct((bsz, lt, LANES), F32),
        ],
        compiler_params=_params(("parallel", "parallel")),
        name="mixer_out_router",
    )(*head_args, x, mod, w_o, b_o, g_ffn, w_r, b_r)


def _route_rank_kernel(idx_ref, rank_ref, cnt_ref, run_ref):
    i = pl.program_id(0)

    @pl.when(i == 0)
    def _():
        run_ref[...] = jnp.zeros_like(run_ref)

    idx = idx_ref[...]
    tm = idx.shape[0]
    lane = lax.broadcasted_iota(jnp.int32, idx.shape, 1)
    earlier = (lax.broadcasted_iota(jnp.int32, (tm, tm), 1) < lax.broadcasted_iota(jnp.int32, (tm, tm), 0)).astype(BF16)
    base = run_ref[...]
    rank = jnp.zeros(idx.shape, jnp.int32)
    for k in range(TOP_K):
        hit = lane == idx[:, k:k + 1]
        before = _dot(earlier, hit.astype(BF16)) + base
        r = jnp.sum(jnp.where(hit, before, 0.0), axis=-1, keepdims=True)
        rank = jnp.where(lane == k, r.astype(jnp.int32), rank)
        base = base + jnp.sum(hit.astype(F32), axis=0, keepdims=True)
    rank_ref[...] = rank
    run_ref[...] = base
    cnt_ref[...] = jnp.broadcast_to(base, cnt_ref.shape)


def _route_rank(idx):
    n = idx.shape[0]
    tm = ROW_TILE
    return pl.pallas_call(
        _route_rank_kernel,
        grid=(n // tm,),
        in_specs=[pl.BlockSpec((tm, LANES), lambda i: (i, 0))],
        out_specs=[pl.BlockSpec((tm, LANES), lambda i: (i, 0)), pl.BlockSpec((8, LANES), lambda i: (0, 0))],
        out_shape=[jax.ShapeDtypeStruct((n, LANES), jnp.int32), jax.ShapeDtypeStruct((8, LANES), F32)],
        scratch_shapes=[pltpu.VMEM((1, LANES), F32)],
        compiler_params=_params(("arbitrary",)),
        name="route_rank",
    )(idx)


def _dispatch_kernel(dest_ref, h_ref, xs_in_ref, xs_ref, sem):
    del xs_in_ref
    tm = h_ref.shape[0]

    def row_copy(r, dst):
        return pltpu.make_async_copy(h_ref.at[pl.ds(r, 1)], xs_ref.at[pl.ds(dst, 1)], sem)

    def issue(r, carry):
        for k in range(TOP_K):
            row_copy(r, dest_ref[0, 0, r * TOP_K + k]).start()
        return carry

    lax.fori_loop(0, tm, issue, 0, unroll=8)
    def drain(r, carry):
        for k in range(TOP_K):
            row_copy(r, 0).wait()
        return carry

    lax.fori_loop(0, tm, drain, 0, unroll=8)


def _dispatch(h2, dest, rows):
    n, d = h2.shape
    tm = ROW_TILE
    dest3 = dest.reshape(n // tm, 1, tm * TOP_K)
    return pl.pallas_call(
        _dispatch_kernel,
        grid=(n // tm,),
        in_specs=[
            pl.BlockSpec((1, 1, tm * TOP_K), lambda i: (i, 0, 0), memory_space=pltpu.SMEM),
            pl.BlockSpec((tm, d), lambda i: (i, 0)),
            pl.BlockSpec(memory_space=pl.ANY),
        ],
        out_specs=pl.BlockSpec(memory_space=pl.ANY),
        out_shape=jax.ShapeDtypeStruct((rows, d), h2.dtype),
        scratch_shapes=[pltpu.SemaphoreType.DMA],
        input_output_aliases={2: 0},
        compiler_params=_params(("arbitrary",)),
        name="moe_dispatch",
    )(dest3, h2, jnp.zeros((rows, d), h2.dtype))


def _moe_kernel(te_ref, nu_ref, x_ref, wu_ref, bu_ref, wd_ref, bd_ref, o_ref):
    i = pl.program_id(0)
    de = wd_ref.shape[1]

    @pl.when(i < nu_ref[0])
    def _():
        u = _dot(x_ref[...].astype(BF16), wu_ref[0]) + bu_ref[0]
        glu = jnp.minimum(u[:, :de], SWIGLU_LIMIT)
        lin = jnp.clip(u[:, de:], -SWIGLU_LIMIT, SWIGLU_LIMIT)
        act = glu * jax.nn.sigmoid(SWIGLU_ALPHA * glu) * (lin + 1.0)
        o_ref[...] = (_dot(act.astype(BF16), wd_ref[0]) + bd_ref[0]).astype(o_ref.dtype)

    @pl.when(i >= nu_ref[0])
    def _():
        o_ref[...] = jnp.zeros_like(o_ref)


def _moe_experts(xs, tile_expert, n_used, w_up, b_up, w_down, b_down):
    rows, d = xs.shape
    tm = MOE_TILE
    n_tiles = rows // tm
    ne, _, n_up = w_up.shape
    de = w_down.shape[1]
    grid_spec = pltpu.PrefetchScalarGridSpec(
        num_scalar_prefetch=2,
        grid=(n_tiles,),
        in_specs=[
            pl.BlockSpec((tm, d), lambda i, te, nu: (jnp.minimum(i, nu[0] - 1), 0)),
            pl.BlockSpec((1, d, n_up), lambda i, te, nu: (te[i], 0, 0)),
            pl.BlockSpec((1, 1, n_up), lambda i, te, nu: (te[i], 0, 0)),
            pl.BlockSpec((1, de, d), lambda i, te, nu: (te[i], 0, 0)),
            pl.BlockSpec((1, 1, d), lambda i, te, nu: (te[i], 0, 0)),
        ],
        out_specs=pl.BlockSpec((tm, d), lambda i, te, nu: (i, 0)),
    )
    return pl.pallas_call(
        _moe_kernel,
        grid_spec=grid_spec,
        out_shape=jax.ShapeDtypeStruct((rows, d), BF16),
        compiler_params=_params(("arbitrary",)),
        name="moe_experts",
    )(tile_expert, n_used, xs, w_up, b_up.reshape(ne, 1, n_up), w_down, b_down.reshape(ne, 1, d))


def _combine_kernel(x_ref, y0_ref, y1_ref, y2_ref, y3_ref, gate_ref, mod_ref, o_ref):
    gates = gate_ref[0]
    acc = gates[:, 0:1] * y0_ref[...].astype(F32)
    for k, y_ref in ((1, y1_ref), (2, y2_ref), (3, y3_ref)):
        acc = acc + gates[:, k:k + 1] * y_ref[...].astype(F32)
    o_ref[0] = x_ref[0] + mod_ref[0, 0][5:6] * acc


def _moe_combine(x, yg, gates, mod, *, n_ctx):
    bsz, lt, d = x.shape
    tm = ROW_TILE
    nct = n_ctx // tm
    nt = lt // tm
    row = lambda i, t: (i, t, 0)
    slot = lambda k: pl.BlockSpec((tm, d), lambda i, t: (k * bsz * nt + i * nt + t, 0))
    return pl.pallas_call(
        _combine_kernel,
        grid=(bsz, nt),
        in_specs=[
            pl.BlockSpec((1, tm, d), row),
            slot(0), slot(1), slot(2), slot(3),
            pl.BlockSpec((1, tm, LANES), row),
            pl.BlockSpec((1, 1, 6, d), lambda i, t: (i, (t >= nct).astype(jnp.int32), 0, 0)),
        ],
        out_specs=pl.BlockSpec((1, tm, d), row),
        out_shape=jax.ShapeDtypeStruct((bsz, lt, d), F32),
        compiler_params=_params(("parallel", "parallel")),
        name="moe_combine",
    )(x, yg, yg, yg, yg, gates, mod)


def _moe_layer(x_mid, h2, idx, gates, mod, w_up, b_up, w_down, b_down, *, n_ctx):
    bsz, lt, d = x_mid.shape
    n = bsz * lt
    tm = MOE_TILE
    n_tiles = -(-(n * TOP_K) // tm) + N_EXPERTS
    idx2 = idx.reshape(n, LANES)
    rank, cnt = _route_rank(idx2)
    counts = cnt[0, :N_EXPERTS].astype(jnp.int32)
    padded = (counts + tm - 1) // tm * tm
    pend = jnp.cumsum(padded)
    pstart = pend - padded
    experts = idx2[:, :TOP_K]
    onehot = experts[:, :, None] == jnp.arange(N_EXPERTS, dtype=jnp.int32)[None, None, :]
    dest = jnp.sum(jnp.where(onehot, pstart[None, None, :], 0), axis=-1) + rank[:, :TOP_K]
    n_used = (pend[-1] // tm).astype(jnp.int32)
    tile_start = jnp.arange(n_tiles, dtype=jnp.int32) * tm
    tile_e = jnp.sum((tile_start[:, None] >= pend[None, :]).astype(jnp.int32), axis=1)
    last_e = jnp.sum((jnp.maximum(n_used - 1, 0) * tm >= pend).astype(jnp.int32))
    tile_e = jnp.minimum(jnp.where(jnp.arange(n_tiles) < n_used, tile_e, last_e), N_EXPERTS - 1)
    xs = _dispatch(h2.reshape(n, d), dest, n_tiles * tm)
    ys = _moe_experts(xs, tile_e, n_used.reshape(1), w_up, b_up, w_down, b_down)
    yg = ys.at[dest.T.reshape(-1)].get(mode='promise_in_bounds')
    return _moe_combine(x_mid, yg, gates, mod, n_ctx=n_ctx)


def _gdn_conv_kernel(p_ref, w_ref, o_ref, *, n_ctx):
    c = pl.program_id(1)
    lt, tc = p_ref.shape[1], p_ref.shape[2]
    x = p_ref[0].astype(F32)
    row = lax.broadcasted_iota(jnp.int32, (lt, 1), 0)
    is_ctx = row < n_ctx
    pos = jnp.where(is_ctx, row, row - n_ctx)
    seg_len = jnp.where(is_ctx, n_ctx, lt - n_ctx)
    acc = x * w_ref[CONV_K // 2:CONV_K // 2 + 1, :]
    for j in range(CONV_K):
        dlt = j - CONV_K // 2
        if dlt == 0:
            continue
        shifted = pltpu.roll(x, (-dlt) % lt, 0)
        valid = (pos + dlt >= 0) & (pos + dlt < seg_len)
        acc = acc + jnp.where(valid, shifted, 0.0) * w_ref[j:j + 1, :]
    y = acc * jax.nn.sigmoid(acc)
    n_qk_tiles = 2 * GDN_HEADS * GDN_DK // tc
    is_qk = c < n_qk_tiles
    scale = jnp.where(c < n_qk_tiles // 2, GDN_DK ** -0.5, 1.0)
    for hd in range(tc // LANES):
        t = y[:, hd * LANES:(hd + 1) * LANES]
        normed = t * lax.rsqrt(jnp.sum(t * t, axis=-1, keepdims=True) + NORM_EPS) * scale
        o_ref[0, :, hd * LANES:(hd + 1) * LANES] = jnp.where(is_qk, normed, t).astype(o_ref.dtype)


def _gdn_conv(p_main, conv_w, *, n_ctx):
    bsz, lt, _ = p_main.shape
    n = conv_w.shape[1]
    tc = 4 * LANES
    return pl.pallas_call(
        functools.partial(_gdn_conv_kernel, n_ctx=n_ctx),
        grid=(bsz, n // tc),
        in_specs=[
            pl.BlockSpec((1, lt, tc), lambda i, c: (i, 0, c)),
            pl.BlockSpec((CONV_K, tc), lambda i, c: (0, c)),
        ],
        out_specs=pl.BlockSpec((1, lt, tc), lambda i, c: (i, 0, c)),
        out_shape=jax.ShapeDtypeStruct((bsz, lt, n), BF16),
        compiler_params=_params(("parallel", "parallel")),
        name="gdn_conv",
    )(p_main, conv_w)


def _gdn_delta_kernel(qf_ref, kf_ref, vf_ref, qb_ref, kb_ref, vb_ref, colf_ref, colb_ref, rowf_ref, rowb_ref,
                      of_ref, ob_ref, s_ref):
    step = pl.program_id(1)

    @pl.when(step == 0)
    def _():
        s_ref[...] = jnp.zeros_like(s_ref)

    ii = lax.broadcasted_iota(jnp.int32, (CHUNK, CHUNK), 0)
    jj = lax.broadcasted_iota(jnp.int32, (CHUNK, CHUNK), 1)
    eye = (ii == jj).astype(F32)
    dirs = (
        (0, qf_ref, kf_ref, vf_ref, colf_ref, rowf_ref, of_ref, ii >= jj, ii > jj, CHUNK - 1),
        (1, qb_ref, kb_ref, vb_ref, colb_ref, rowb_ref, ob_ref, ii <= jj, ii < jj, 0),
    )
    chains = []
    for d, q_ref, k_ref, v_ref, col_ref, row_ref, o_ref, tri, strict, last in dirs:
        col = col_ref[0, 0]
        rows = row_ref[0, 0]
        for hd in range(GDN_HEADS):
            gi = d * GDN_HEADS + hd
            chains.append(dict(
                gi=gi, cols=slice(hd * LANES, (hd + 1) * LANES), q_ref=q_ref, k_ref=k_ref, v_ref=v_ref, o_ref=o_ref,
                tri=tri, strict=strict, last=last,
                gc=col[:, gi:gi + 1], beta=col[:, 2 * GDN_HEADS + gi:2 * GDN_HEADS + gi + 1], gr=rows[gi:gi + 1, :]))

    for c in chains:
        k = c['k_ref'][0, :, c['cols']]
        c['k16'] = k
        c['kbeta'] = k.astype(F32) * c['beta']
        c['decay'] = jnp.where(c['tri'], jnp.exp(jnp.where(c['tri'], c['gc'] - c['gr'], 0.0)), 0.0)
    for c in chains:
        c['p'] = jnp.where(c['strict'], _dot_nt(c['kbeta'].astype(BF16), c['k16']) * c['decay'], 0.0)
        c['x'] = eye - c['p']
    for _ in range((CHUNK - 1).bit_length() - 1):
        for c in chains:
            p16 = c['p'].astype(BF16)
            c['p'] = _dot(p16, p16)
        for c in chains:
            c['x'] = c['x'] + _dot(c['x'].astype(BF16), c['p'].astype(BF16))
    for c in chains:
        t_inv = c['x'].astype(BF16)
        v = c['v_ref'][0, :, c['cols']].astype(F32)
        c['u'] = _dot(t_inv, (v * c['beta']).astype(BF16))
        c['w'] = _dot(t_inv, (c['kbeta'] * jnp.exp(c['gc'])).astype(BF16))
    for c in chains:
        q = c['q_ref'][0, :, c['cols']]
        c['intra'] = (_dot_nt(q, c['k16']) * c['decay']).astype(BF16)
        c['qg'] = (q.astype(F32) * jnp.exp(c['gc'])).astype(BF16)
    for c in chains:
        c['s'] = s_ref[c['gi']]
        c['s16'] = c['s'].astype(BF16)
        c['v_new'] = (c['u'] - _dot(c['w'].astype(BF16), c['s16'])).astype(BF16)
    for c in chains:
        c['o_ref'][0, :, c['cols']] = _dot(c['qg'], c['s16']) + _dot(c['intra'], c['v_new'])
    for c in chains:
        g_last = c['gc'][c['last']:c['last'] + 1, :]
        k_dec = (c['k16'].astype(F32) * jnp.exp(g_last - c['gc'])).astype(BF16)
        s_ref[c['gi']] = c['s'] * jnp.exp(g_last) + _dot_tn(k_dec, c['v_new'])


def _gdn_delta(qkv, col_pack, row_pack, *, n_ctx):
    bsz, lt, _ = qkv.shape
    nc = lt // CHUNK
    ncc = n_ctx // CHUNK
    kw = GDN_HEADS * GDN_DK

    def bwd_chunk(c):
        return jnp.where(c < ncc, ncc - 1 - c, nc - 1 - (c - ncc))

    fwd = lambda blk: pl.BlockSpec((1, CHUNK, kw), lambda i, c: (i, c, blk))
    bwd = lambda blk: pl.BlockSpec((1, CHUNK, kw), lambda i, c: (i, bwd_chunk(c), blk))
    ncol, nrow = col_pack.shape[3], row_pack.shape[2]
    return pl.pallas_call(
        _gdn_delta_kernel,
        grid=(bsz, nc),
        in_specs=[
            fwd(0), fwd(1), fwd(2), bwd(0), bwd(1), bwd(2),
            pl.BlockSpec((1, 1, CHUNK, ncol), lambda i, c: (i, c, 0, 0)),
            pl.BlockSpec((1, 1, CHUNK, ncol), lambda i, c: (i, bwd_chunk(c), 0, 0)),
            pl.BlockSpec((1, 1, nrow, CHUNK), lambda i, c: (i, c, 0, 0)),
            pl.BlockSpec((1, 1, nrow, CHUNK), lambda i, c: (i, bwd_chunk(c), 0, 0)),
        ],
        out_specs=[
            pl.BlockSpec((1, CHUNK, kw), lambda i, c: (i, c, 0)),
            pl.BlockSpec((1, CHUNK, kw), lambda i, c: (i, bwd_chunk(c), 0)),
        ],
        out_shape=[jax.ShapeDtypeStruct((bsz, lt, kw), F32), jax.ShapeDtypeStruct((bsz, lt, kw), F32)],
        scratch_shapes=[pltpu.VMEM((2 * GDN_HEADS, GDN_DK, GDN_DK), F32)],
        compiler_params=_params(("parallel", "arbitrary")),
        name="gdn_delta",
    )(qkv, qkv, qkv, qkv, qkv, qkv, col_pack, col_pack, row_pack, row_pack)


def _gdn_gates(p_small, a_log, dt_bias):
    bsz, lt, _ = p_small.shape
    nh = GDN_HEADS
    beta = jax.nn.sigmoid(p_small[..., :2 * nh])
    a = p_small[..., 2 * nh:4 * nh]
    g = -jnp.exp(a_log.reshape(2 * nh)) * jax.nn.softplus(a + dt_bias.reshape(2 * nh))
    gch = g.reshape(bsz, lt // CHUNK, CHUNK, 2 * nh)
    gf = jnp.cumsum(gch[..., :nh], axis=2)
    gb = jnp.flip(jnp.cumsum(jnp.flip(gch[..., nh:], axis=2), axis=2), axis=2)
    gc = jnp.concatenate([gf, gb], axis=-1)
    col_pack = jnp.concatenate([gc, beta.reshape(bsz, lt // CHUNK, CHUNK, 2 * nh)], axis=-1)
    row_pack = jnp.swapaxes(gc, 2, 3)
    return col_pack, row_pack


def _final_norm_kernel(x_ref, g_ref, o_ref):
    o_ref[0] = _rms(x_ref[0]) * g_ref[...]


def _final_norm(x, g, *, n_ctx):
    bsz, lt, d = x.shape
    tm = ROW_TILE
    nct = n_ctx // tm
    return pl.pallas_call(
        _final_norm_kernel,
        grid=(bsz, (lt - n_ctx) // tm),
        in_specs=[pl.BlockSpec((1, tm, d), lambda i, t: (i, t + nct, 0)), pl.BlockSpec((1, d), lambda i, t: (0, 0))],
        out_specs=pl.BlockSpec((1, tm, d), lambda i, t: (i, t, 0)),
        out_shape=jax.ShapeDtypeStruct((bsz, lt - n_ctx, d), F32),
        compiler_params=_params(("parallel", "parallel")),
        name="final_norm",
    )(x, g)


def _rope_tables(n_ctx, seq):
    half = AXIS_DIM // 2
    lane = jnp.arange(LANES)
    inv = ROPE_THETA ** (-jnp.arange(0, AXIS_DIM, 2, dtype=F32) / AXIS_DIM)
    freq = inv[lane % half]
    use_col = (lane % HEAD_DIM) // AXIS_DIM == 1
    sign = jnp.where((lane // half) % 2 == 0, -1.0, 1.0)
    tok = jnp.arange(seq)
    pos = jnp.where(use_col[None, :], (tok % GRID_W)[:, None], (tok // GRID_W)[:, None]).astype(F32)
    ang = pos * freq[None, :]
    cos = jnp.concatenate([jnp.ones((n_ctx, LANES), F32), jnp.cos(ang)], axis=0)
    sin = jnp.concatenate([jnp.zeros((n_ctx, LANES), F32), jnp.sin(ang) * sign[None, :]], axis=0)
    return cos, sin


def _pad_cols(a, n):
    return jnp.pad(a, [(0, 0)] * (a.ndim - 1) + [(0, n - a.shape[-1])])


def kernel(x, c, ctx, c_ctx, w_mod, b_mod, g_mix, g_ffn, win_w_qkv, win_b_qkv, win_sink, win_w_o, win_b_o, glb_w_qkv, glb_g_q, glb_g_k, glb_w_o, gdn_w_in, gdn_conv_w, gdn_a_log, gdn_dt_bias, gdn_g_out, gdn_w_o, moe_w_router, moe_b_router, moe_w_up, moe_b_up, moe_w_down, moe_b_down, g_final):
    bsz, seq, d = x.shape
    n_ctx = ctx.shape[1]
    depth = w_mod.shape[0]
    assert n_ctx % ROW_TILE == 0 and seq % ROW_TILE == 0 and n_ctx >= WINDOW and seq >= ATTN_TQ + 2 * WINDOW
    nq, nk = N_HEADS * HEAD_DIM, N_KV_HEADS * HEAD_DIM

    xs = jnp.concatenate([ctx, x], axis=1)
    cond_rows = -(-(bsz + 1) // 8) * 8
    cond = jnp.zeros((cond_rows, d), F32).at[:bsz].set(c).at[bsz].set(c_ctx)
    mods = _adaln(cond, w_mod, b_mod).reshape(depth, cond_rows, 6, d)
    cos, sin = _rope_tables(n_ctx, seq)
    zero_bias = jnp.zeros((1, d), F32)

    for i in range(depth):
        kind, j = i % 3, i // 3
        mod = jnp.stack([jnp.broadcast_to(mods[i, bsz], (bsz, 6, d)), mods[i, :bsz]], axis=1)
        g1 = g_mix[i].reshape(1, d)
        w_r = _pad_cols(moe_w_router[i], LANES)
        b_r = jnp.concatenate([moe_b_router[i], jnp.full((LANES - N_EXPERTS,), NEG_INF, F32)]).reshape(1, LANES)
        route_args = (g_ffn[i].reshape(1, d), w_r, b_r)
        if kind == 0:
            qkv = _qkv_project(xs, mod, g1, win_w_qkv[j].astype(BF16), win_b_qkv[j].reshape(1, -1), cos, sin,
                               jnp.ones((1, nq + nk), F32), n_ctx=n_ctx, qk_norm=False)
            o = _attention(qkv, win_sink[j], windowed=True, n_ctx=n_ctx)
            outs = _mixer_out((o,), xs, mod, win_w_o[j].astype(BF16), win_b_o[j].reshape(1, d), *route_args,
                              n_ctx=n_ctx)
        elif kind == 1:
            hg = jnp.concatenate([jnp.tile(glb_g_q[j], N_HEADS), jnp.tile(glb_g_k[j], N_KV_HEADS)]).reshape(1, -1)
            qkv = _qkv_project(xs, mod, g1, glb_w_qkv[j].astype(BF16), jnp.zeros((1, nq + 2 * nk), F32), cos, sin,
                               hg, n_ctx=n_ctx, qk_norm=True)
            o = _attention(qkv, jnp.zeros((N_HEADS,), F32), windowed=False, n_ctx=n_ctx)
            outs = _mixer_out((o,), xs, mod, glb_w_o[j].astype(BF16), zero_bias, *route_args, n_ctx=n_ctx)
        else:
            n_main = 2 * GDN_HEADS * GDN_DK + 2 * GDN_HEADS * LANES
            w_in = _pad_cols(gdn_w_in[j], n_main + LANES).astype(BF16)
            p_main, p_small = _gdn_in_project(xs, mod, g1, w_in, n_ctx=n_ctx, n_main=n_main)
            qkv = _gdn_conv(p_main, gdn_conv_w[j], n_ctx=n_ctx)
            col_pack, row_pack = _gdn_gates(p_small, gdn_a_log[j], gdn_dt_bias[j])
            o_f, o_b = _gdn_delta(qkv, col_pack, row_pack, n_ctx=n_ctx)
            outs = _mixer_out((o_f, o_b, p_main), xs, mod, gdn_w_o[j].astype(BF16), zero_bias, *route_args,
                              n_ctx=n_ctx, gdn_gain=gdn_g_out[j].reshape(1, LANES))
        x_mid, h2, idx, gates = outs
        xs = _moe_layer(x_mid, h2, idx, gates, mod, moe_w_up[i].astype(BF16), moe_b_up[i],
                        moe_w_down[i].astype(BF16), moe_b_down[i], n_ctx=n_ctx)
    return _final_norm(xs, g_final.reshape(1, d), n_ctx=n_ctx)
```

```python
import functools

import jax
import jax.numpy as jnp
from jax import lax
from jax.experimental import pallas as pl
from jax.experimental.pallas import tpu as pltpu

F32 = jnp.float32
BF16 = jnp.bfloat16
HIGHEST = lax.Precision.HIGHEST

GRID_W = 64
HEAD_DIM = 64
N_HEADS = 16
N_KV_HEADS = 4
WINDOW = 128
ROPE_THETA = 10000.0
AXIS_DIM = HEAD_DIM // 2
GDN_HEADS = 8
GDN_DK = 128
CONV_K = 5
CHUNK = 64
N_EXPERTS = 32
TOP_K = 4
SWIGLU_LIMIT = 7.0
SWIGLU_ALPHA = 1.702
NORM_EPS = 1e-6
NEG_INF = -1e30

LANES = 128
ROW_TILE = 256
ATTN_TQ = 128
MIX_SUB = 128
MOE_TILE = 256
VMEM_LIMIT = 48 * 1024 * 1024
MOE_VMEM_LIMIT = 56 * 1024 * 1024


def _params(sem):
    return pltpu.CompilerParams(dimension_semantics=sem, vmem_limit_bytes=VMEM_LIMIT)


def _rms(x, eps=NORM_EPS):
    return x * lax.rsqrt(jnp.mean(x * x, axis=-1, keepdims=True) + eps)


def _dot(a, b):
    return jnp.dot(a, b, preferred_element_type=F32)


def _dot_nt(a, b):
    return lax.dot_general(a, b, (((1,), (1,)), ((), ())), preferred_element_type=F32)


def _dot_tn(a, b):
    return lax.dot_general(a, b, (((0,), (0,)), ((), ())), preferred_element_type=F32)


def _adaln_kernel(cond_ref, w_ref, b_ref, o_ref):
    c = cond_ref[...]
    a = (c * jax.nn.sigmoid(c)).astype(BF16)
    o_ref[0] = _dot(a, w_ref[0].astype(BF16)) + b_ref[0]


def _adaln(cond, w_mod, b_mod):
    depth, d, n = w_mod.shape
    r = cond.shape[0]
    tn = n // 4
    return pl.pallas_call(
        _adaln_kernel,
        grid=(depth, n // tn),
        in_specs=[
            pl.BlockSpec((r, d), lambda i, j: (0, 0)),
            pl.BlockSpec((1, d, tn), lambda i, j: (i, 0, j)),
            pl.BlockSpec((1, 1, tn), lambda i, j: (i, 0, j)),
        ],
        out_specs=pl.BlockSpec((1, r, tn), lambda i, j: (i, 0, j)),
        out_shape=jax.ShapeDtypeStruct((depth, r, n), F32),
        compiler_params=_params(("parallel", "parallel")),
        name="adaln",
    )(cond, w_mod, b_mod.reshape(depth, 1, n))


def _norm_mod(x, m, g, which):
    return _rms(x) * g * (1.0 + m[which + 1:which + 2]) + m[which:which + 1]


def _swap16(t):
    lane = lax.broadcasted_iota(jnp.int32, t.shape, 1)
    first = (lane // (AXIS_DIM // 2)) % 2 == 0
    return jnp.where(first, pltpu.roll(t, LANES - AXIS_DIM // 2, 1), pltpu.roll(t, AXIS_DIM // 2, 1))


def _head_sum_matrix():
    r = lax.broadcasted_iota(jnp.int32, (LANES, LANES), 0) // HEAD_DIM
    c = lax.broadcasted_iota(jnp.int32, (LANES, LANES), 1) // HEAD_DIM
    return (r == c).astype(F32)


def _qkv_kernel(x_ref, mod_ref, g_ref, w_ref, b_ref, cos_ref, sin_ref, hg_ref, o_ref, *, qk_norm):
    nq = N_HEADS * HEAD_DIM
    nk = N_KV_HEADS * HEAD_DIM
    h = _norm_mod(x_ref[0], mod_ref[0, 0], g_ref[...], 0)
    p = _dot(h.astype(BF16), w_ref[...]) + b_ref[...]
    cos = cos_ref[...]
    sin = sin_ref[...]
    lane = lax.broadcasted_iota(jnp.int32, cos.shape, 1)
    low = lane < HEAD_DIM
    if qk_norm:
        seg = _head_sum_matrix()

    def expand(t, base, ones_lane):
        r = pltpu.roll(t, HEAD_DIM, 1)
        fill_hi = jnp.where(lane == HEAD_DIM, 1.0, 0.0) if ones_lane else jnp.zeros_like(t)
        fill_lo = jnp.where(lane == 0, 1.0, 0.0) if ones_lane else jnp.zeros_like(t)
        parts = (jnp.where(low, t, fill_hi), jnp.where(low, fill_lo, r),
                 jnp.where(low, r, fill_hi), jnp.where(low, fill_lo, t))
        for i, part in enumerate(parts):
            o_ref[0, :, base + i * LANES: base + (i + 1) * LANES] = part.astype(o_ref.dtype)

    for j in range((nq + nk) // LANES):
        t = p[:, j * LANES:(j + 1) * LANES]
        if qk_norm:
            ss = jnp.dot(t * t, seg, precision=HIGHEST, preferred_element_type=F32)
            t = t * lax.rsqrt(ss * (1.0 / HEAD_DIM) + NORM_EPS) * hg_ref[:, j * LANES:(j + 1) * LANES]
        t = t * cos + _swap16(t) * sin
        if j < nq // LANES:
            o_ref[0, :, j * LANES:(j + 1) * LANES] = (t * (HEAD_DIM ** -0.5)).astype(o_ref.dtype)
        else:
            expand(t, nq + (j - nq // LANES) * 4 * LANES, False)
    for j in range(nk // LANES):
        t = p[:, nq + nk + j * LANES: nq + nk + (j + 1) * LANES]
        expand(t, nq + 4 * nk + j * 4 * LANES, True)


def _qkv_project(x, mod, g, w, b, cos, sin, hg, *, n_ctx, qk_norm):
    bsz, lt, d = x.shape
    n = w.shape[1]
    tm = ROW_TILE
    nct = n_ctx // tm
    n_out = N_HEADS * HEAD_DIM + 8 * N_KV_HEADS * HEAD_DIM
    return pl.pallas_call(
        functools.partial(_qkv_kernel, qk_norm=qk_norm),
        grid=(bsz, lt // tm),
        in_specs=[
            pl.BlockSpec((1, tm, d), lambda i, t: (i, t, 0)),
            pl.BlockSpec((1, 1, 6, d), lambda i, t: (i, (t >= nct).astype(jnp.int32), 0, 0)),
            pl.BlockSpec((1, d), lambda i, t: (0, 0)),
            pl.BlockSpec((d, n), lambda i, t: (0, 0)),
            pl.BlockSpec((1, n), lambda i, t: (0, 0)),
            pl.BlockSpec((tm, LANES), lambda i, t: (t, 0)),
            pl.BlockSpec((tm, LANES), lambda i, t: (t, 0)),
            pl.BlockSpec((1, hg.shape[1]), lambda i, t: (0, 0)),
        ],
        out_specs=pl.BlockSpec((1, tm, n_out), lambda i, t: (i, t, 0)),
        out_shape=jax.ShapeDtypeStruct((bsz, lt, n_out), BF16),
        compiler_params=_params(("parallel", "parallel")),
        name="qkv_project",
    )(x, mod, g, w, b, cos, sin, hg)


def _gdn_in_kernel(x_ref, mod_ref, g_ref, w_ref, o_ref, o2_ref):
    h = _norm_mod(x_ref[0], mod_ref[0, 0], g_ref[...], 0)
    p = _dot(h.astype(BF16), w_ref[...])
    n_main = o_ref.shape[2]
    o_ref[0] = p[:, :n_main].astype(o_ref.dtype)
    o2_ref[0] = p[:, n_main:]


def _gdn_in_project(x, mod, g, w, *, n_ctx, n_main):
    bsz, lt, d = x.shape
    n = w.shape[1]
    tm = ROW_TILE
    nct = n_ctx // tm
    return pl.pallas_call(
        _gdn_in_kernel,
        grid=(bsz, lt // tm),
        in_specs=[
            pl.BlockSpec((1, tm, d), lambda i, t: (i, t, 0)),
            pl.BlockSpec((1, 1, 6, d), lambda i, t: (i, (t >= nct).astype(jnp.int32), 0, 0)),
            pl.BlockSpec((1, d), lambda i, t: (0, 0)),
            pl.BlockSpec((d, n), lambda i, t: (0, 0)),
        ],
        out_specs=[
            pl.BlockSpec((1, tm, n_main), lambda i, t: (i, t, 0)),
            pl.BlockSpec((1, tm, n - n_main), lambda i, t: (i, t, 0)),
        ],
        out_shape=[
            jax.ShapeDtypeStruct((bsz, lt, n_main), BF16),
            jax.ShapeDtypeStruct((bsz, lt, n - n_main), F32),
        ],
        compiler_params=_params(("parallel", "parallel")),
        name="gdn_in_project",
    )(x, mod, g, w)


def _attn_kernel(sink_ref, q_ref, k_ref, v_ref, o_ref, *, windowed, n_ctx, seq):
    tq = q_ref.shape[1]
    t = pl.program_id(1)
    nct = n_ctx // tq
    gw = 2 * LANES
    top = lax.broadcasted_iota(jnp.int32, (2 * tq, 1), 0) < tq
    low = lax.broadcasted_iota(jnp.int32, (2 * tq, LANES), 1) < HEAD_DIM

    def run(segments, batch):
        for g0 in range(0, N_KV_HEADS, batch):
            heads = range(g0, g0 + batch)
            jobs = [(g, half) for g in heads for half in range(2)]
            qs = {g: jnp.concatenate([q_ref[0, :, g * gw:g * gw + LANES],
                                      q_ref[0, :, g * gw + LANES:(g + 1) * gw]], axis=0) for g in heads}
            scores, ms, outs = {}, {}, {}
            for g, half in jobs:
                cols = slice(g * gw + half * LANES, g * gw + (half + 1) * LANES)
                scores[g, half] = []
                for rows, mask in segments:
                    s = _dot_nt(qs[g], k_ref[0, rows, cols])
                    if mask is not None:
                        s = jnp.where(mask, s, NEG_INF)
                    scores[g, half].append(s)
            for g, half in jobs:
                m = scores[g, half][0].max(axis=-1, keepdims=True)
                for s in scores[g, half][1:]:
                    m = jnp.maximum(m, s.max(axis=-1, keepdims=True))
                if windowed:
                    m = jnp.maximum(m, jnp.where(top, sink_ref[g * 4 + half], sink_ref[g * 4 + 2 + half]))
                ms[g, half] = m
            for g, half in jobs:
                cols = slice(g * gw + half * LANES, g * gw + (half + 1) * LANES)
                o = None
                for s, (rows, _) in zip(scores[g, half], segments):
                    p = jnp.exp((s - ms[g, half]).astype(BF16))
                    pv = _dot(p, v_ref[0, rows, cols])
                    o = pv if o is None else o + pv
                outs[g, half] = o
            for g in heads:
                den_lo = outs[g, 0][:, HEAD_DIM:HEAD_DIM + 1]
                den_hi = outs[g, 1][:, 0:1]
                if windowed:
                    den_lo = den_lo + jnp.exp(jnp.where(top, sink_ref[g * 4], sink_ref[g * 4 + 2]) - ms[g, 0])
                    den_hi = den_hi + jnp.exp(jnp.where(top, sink_ref[g * 4 + 1], sink_ref[g * 4 + 3]) - ms[g, 1])
                acc = jnp.where(low, outs[g, 0] * (1.0 / den_lo), outs[g, 1] * (1.0 / den_hi))
                o_ref[0, :, g * gw:g * gw + LANES] = acc[:tq].astype(o_ref.dtype)
                o_ref[0, :, g * gw + LANES:(g + 1) * gw] = acc[tq:].astype(o_ref.dtype)

    ctx_rows = slice(0, n_ctx)

    @pl.when(t < nct)
    def _():
        run([(ctx_rows, None)], N_KV_HEADS)

    @pl.when(t >= nct)
    def _():
        if windowed:
            span = tq + 2 * WINDOW
            start = (t - nct) * tq
            r0 = pl.multiple_of(jnp.minimum(n_ctx + start - WINDOW, n_ctx + seq - span), LANES)
            qpos = start + lax.broadcasted_iota(jnp.int32, (tq, span), 0)
            kpos = r0 - n_ctx + lax.broadcasted_iota(jnp.int32, (tq, span), 1)
            band = (jnp.abs(qpos - kpos) <= WINDOW) & (kpos >= 0)
            band = jnp.concatenate([band, band], axis=0)
            run([(pl.ds(r0, span), band), (ctx_rows, None)], N_KV_HEADS)
        else:
            run([(slice(0, n_ctx + seq), None)], 1)


def _attention(qkv, sink, *, windowed, n_ctx):
    bsz, lt, _ = qkv.shape
    tq = ATTN_TQ
    nq = N_HEADS * HEAD_DIM
    kernel = functools.partial(_attn_kernel, windowed=windowed, n_ctx=n_ctx, seq=lt - n_ctx)
    return pl.pallas_call(
        kernel,
        grid=(bsz, lt // tq),
        in_specs=[
            pl.BlockSpec(memory_space=pltpu.SMEM),
            pl.BlockSpec((1, tq, nq), lambda i, t: (i, t, 0)),
            pl.BlockSpec((1, lt, nq), lambda i, t: (i, 0, 1)),
            pl.BlockSpec((1, lt, nq), lambda i, t: (i, 0, 2)),
        ],
        out_specs=pl.BlockSpec((1, tq, nq), lambda i, t: (i, t, 0)),
        out_shape=jax.ShapeDtypeStruct((bsz, lt, nq), BF16),
        compiler_params=_params(("parallel", "arbitrary")),
        name="window_attention" if windowed else "global_attention",
    )(sink, qkv, qkv, qkv)


def _sub_tiles(tm):
    return [slice(r, r + MIX_SUB) for r in range(0, tm, MIX_SUB)]


def _mix_out_tail(ys, x_ref, m, gffn_ref, wr_ref, br_ref, xmid_ref, h2_ref, idx_ref, gate_ref):
    g = gffn_ref[...]
    w_hi, w_lo, b_r = wr_ref[0], wr_ref[1], br_ref[...]
    hs = []
    for rows, y in ys:
        xm = x_ref[0, rows, :] + m[2:3] * y
        xmid_ref[0, rows, :] = xm
        h = _norm_mod(xm, m, g, 3)
        h2_ref[0, rows, :] = h
        hs.append(h)
    logits = []
    for h in hs:
        h_hi = h.astype(BF16)
        h_lo = (h - h_hi.astype(F32)).astype(BF16)
        logits.append(_dot(h_hi, w_hi) + (_dot(h_hi, w_lo) + _dot(h_lo, w_hi)) + b_r)
    lane = lax.broadcasted_iota(jnp.int32, logits[0].shape, 1)
    vals = [[] for _ in ys]
    idxs = [[] for _ in ys]
    for _ in range(TOP_K):
        for s in range(len(ys)):
            mx = logits[s].max(axis=-1, keepdims=True)
            ix = jnp.where(logits[s] == mx, lane, LANES).min(axis=-1, keepdims=True)
            vals[s].append(mx)
            idxs[s].append(ix)
            logits[s] = jnp.where(lane == ix, -jnp.inf, logits[s])
    for s, (rows, _) in enumerate(ys):
        es = [jnp.exp(v - vals[s][0]) for v in vals[s]]
        inv = 1.0 / (es[0] + es[1] + es[2] + es[3])
        gates = jnp.zeros(lane.shape, F32)
        idx = jnp.zeros(lane.shape, jnp.int32)
        for k in range(TOP_K):
            gates = jnp.where(lane == k, es[k] * inv, gates)
            idx = jnp.where(lane == k, idxs[s][k], idx)
        idx_ref[0, rows, :] = idx
        gate_ref[0, rows, :] = gates


def _attn_out_kernel(o_ref, x_ref, mod_ref, wo_ref, bo_ref, gffn_ref, wr_ref, br_ref,
                     xmid_ref, h2_ref, idx_ref, gate_ref):
    ys = [(rows, _dot(o_ref[0, rows, :], wo_ref[...]) + bo_ref[...]) for rows in _sub_tiles(o_ref.shape[1])]
    _mix_out_tail(ys, x_ref, mod_ref[0, 0], gffn_ref, wr_ref, br_ref, xmid_ref, h2_ref, idx_ref, gate_ref)


def _gdn_out_kernel(of_ref, ob_ref, z_ref, gout_ref, x_ref, mod_ref, wo_ref, bo_ref, gffn_ref, wr_ref, br_ref,
                    xmid_ref, h2_ref, idx_ref, gate_ref):
    gout = gout_ref[...]
    ys = []
    for rows in _sub_tiles(of_ref.shape[1]):
        parts = []
        for hd in range(GDN_HEADS):
            cols = slice(hd * LANES, (hd + 1) * LANES)
            z = z_ref[0, rows, cols].astype(F32)
            o = _rms(of_ref[0, rows, cols] + ob_ref[0, rows, cols]) * gout
            parts.append((o * (z * jax.nn.sigmoid(z))).astype(BF16))
        ys.append((rows, _dot(jnp.concatenate(parts, axis=1), wo_ref[...]) + bo_ref[...]))
    _mix_out_tail(ys, x_ref, mod_ref[0, 0], gffn_ref, wr_ref, br_ref, xmid_ref, h2_ref, idx_ref, gate_ref)


def _mixer_out(o_inputs, x, mod, w_o, b_o, g_ffn, w_r, b_r, *, n_ctx, gdn_gain=None):
    bsz, lt, d = x.shape
    tm = ROW_TILE
    nct = n_ctx // tm
    row = lambda i, t: (i, t, 0)
    const = lambda i, t: (0, 0)
    if gdn_gain is None:
        kernel = _attn_out_kernel
        head_specs = [pl.BlockSpec((1, tm, d), row)]
        head_args = list(o_inputs)
    else:
        kernel = _gdn_out_kernel
        o_f, o_b, p_main = o_inputs
        zblk = p_main.shape[2] // d - 1
        head_specs = [pl.BlockSpec((1, tm, d), row), pl.BlockSpec((1, tm, d), row),
                      pl.BlockSpec((1, tm, d), lambda i, t: (i, t, zblk)),
                      pl.BlockSpec((1, LANES), const)]
        head_args = [o_f, o_b, p_main, gdn_gain]
    return pl.pallas_call(
        kernel,
        grid=(bsz, lt // tm),
        in_specs=head_specs + [
            pl.BlockSpec((1, tm, d), row),
            pl.BlockSpec((1, 1, 6, d), lambda i, t: (i, (t >= nct).astype(jnp.int32), 0, 0)),
            pl.BlockSpec((d, d), const),
            pl.BlockSpec((1, d), const),
            pl.BlockSpec((1, d), const),
            pl.BlockSpec((2, d, LANES), lambda i, t: (0, 0, 0)),
            pl.BlockSpec((1, LANES), const),
        ],
        out_specs=[
            pl.BlockSpec((1, tm, d), row),
            pl.BlockSpec((1, tm, d), row),
            pl.BlockSpec((1, tm, LANES), row),
            pl.BlockSpec((1, tm, LANES), row),
        ],
        out_shape=[
            jax.ShapeDtypeStruct((bsz, lt, d), F32),
            jax.ShapeDtypeStruct((bsz, lt, d), F32),
            jax.ShapeDtypeStruct((bsz, lt, LANES), jnp.int32),
            jax.ShapeDtypeStruct((bsz, lt, LANES), F32),
        ],
        compiler_params=_params(("parallel", "parallel")),
        name="mixer_out_router",
    )(*head_args, x, mod, w_o, b_o, g_ffn, w_r, b_r)


def _route_rank_kernel(idx_ref, rank_ref, cnt_ref, run_ref):
    i = pl.program_id(0)

    @pl.when(i == 0)
    def _():
        run_ref[...] = jnp.zeros_like(run_ref)

    idx = idx_ref[...]
    tm = idx.shape[0]
    lane = lax.broadcasted_iota(jnp.int32, idx.shape, 1)
    earlier = (lax.broadcasted_iota(jnp.int32, (tm, tm), 1) < lax.broadcasted_iota(jnp.int32, (tm, tm), 0)).astype(BF16)
    base = run_ref[...]
    rank = jnp.zeros(idx.shape, jnp.int32)
    for k in range(TOP_K):
        hit = lane == idx[:, k:k + 1]
        before = _dot(earlier, hit.astype(BF16)) + base
        r = jnp.sum(jnp.where(hit, before, 0.0), axis=-1, keepdims=True)
        rank = jnp.where(lane == k, r.astype(jnp.int32), rank)
        base = base + jnp.sum(hit.astype(F32), axis=0, keepdims=True)
    rank_ref[...] = rank
    run_ref[...] = base
    cnt_ref[...] = jnp.broadcast_to(base, cnt_ref.shape)


def _route_rank(idx):
    n = idx.shape[0]
    tm = ROW_TILE
    return pl.pallas_call(
        _route_rank_kernel,
        grid=(n // tm,),
        in_specs=[pl.BlockSpec((tm, LANES), lambda i: (i, 0))],
        out_specs=[pl.BlockSpec((tm, LANES), lambda i: (i, 0)), pl.BlockSpec((8, LANES), lambda i: (0, 0))],
        out_shape=[jax.ShapeDtypeStruct((n, LANES), jnp.int32), jax.ShapeDtypeStruct((8, LANES), F32)],
        scratch_shapes=[pltpu.VMEM((1, LANES), F32)],
        compiler_params=_params(("arbitrary",)),
        name="route_rank",
    )(idx)


def _dispatch_kernel(dest_ref, h_ref, xs_in_ref, xs_ref, sem):
    del xs_in_ref
    tm = h_ref.shape[0]

    def row_copy(r, dst):
        return pltpu.make_async_copy(h_ref.at[pl.ds(r, 1)], xs_ref.at[pl.ds(dst, 1)], sem)

    def issue(r, carry):
        for k in range(TOP_K):
            row_copy(r, dest_ref[0, 0, r * TOP_K + k]).start()
        return carry

    lax.fori_loop(0, tm, issue, 0, unroll=8)
    def drain(r, carry):
        for k in range(TOP_K):
            row_copy(r, 0).wait()
        return carry

    lax.fori_loop(0, tm, drain, 0, unroll=8)


def _dispatch(h2, dest, xs_buf):
    n, d = h2.shape
    rows = xs_buf.shape[0]
    tm = ROW_TILE
    dest3 = dest.reshape(n // tm, 1, tm * TOP_K)
    return pl.pallas_call(
        _dispatch_kernel,
        grid=(n // tm,),
        in_specs=[
            pl.BlockSpec((1, 1, tm * TOP_K), lambda i: (i, 0, 0), memory_space=pltpu.SMEM),
            pl.BlockSpec((tm, d), lambda i: (i, 0)),
            pl.BlockSpec(memory_space=pl.ANY),
        ],
        out_specs=pl.BlockSpec(memory_space=pl.ANY),
        out_shape=jax.ShapeDtypeStruct((rows, d), h2.dtype),
        scratch_shapes=[pltpu.SemaphoreType.DMA],
        input_output_aliases={2: 0},
        compiler_params=_params(("arbitrary",)),
        name="moe_dispatch",
    )(dest3, h2, xs_buf)


def _moe_kernel(te_ref, nu_ref, first_ref, nxt_ref, slot_ref, x_ref, bu_ref, bd_ref, wu_hbm, wd_hbm, o_ref,
                wu_f32, wd_f32, wu16, wd16, sem):
    i = pl.program_id(0)
    de = wd16.shape[0]

    def weight_copies(expert, slot):
        return (pltpu.make_async_copy(wu_hbm.at[expert], wu_f32.at[slot], sem.at[0, slot]),
                pltpu.make_async_copy(wd_hbm.at[expert], wd_f32.at[slot], sem.at[1, slot]))

    active = i < nu_ref[0]

    @pl.when(active & (i == 0))
    def _():
        for cp in weight_copies(te_ref[0], 0):
            cp.start()

    @pl.when(active & (first_ref[i] == 1))
    def _():
        slot = slot_ref[i]
        for cp in weight_copies(te_ref[i], slot):
            cp.wait()

        @pl.when(nxt_ref[i] >= 0)
        def _():
            for cp in weight_copies(nxt_ref[i], 1 - slot):
                cp.start()

        wu16[...] = wu_f32[slot].astype(BF16)
        wd16[...] = wd_f32[slot].astype(BF16)

    @pl.when(active)
    def _():
        u = _dot(x_ref[...].astype(BF16), wu16[...]) + bu_ref[0]
        glu = jnp.minimum(u[:, :de], SWIGLU_LIMIT)
        lin = jnp.clip(u[:, de:], -SWIGLU_LIMIT, SWIGLU_LIMIT)
        act = glu * jax.nn.sigmoid(SWIGLU_ALPHA * glu) * (lin + 1.0)
        o_ref[...] = (_dot(act.astype(BF16), wd16[...]) + bd_ref[0]).astype(o_ref.dtype)

    @pl.when(jnp.logical_not(active))
    def _():
        o_ref[...] = jnp.zeros_like(o_ref)


def _moe_experts(xs, tile_expert, n_used, first, nxt, slot, w_up, b_up, w_down, b_down):
    rows, d = xs.shape
    tm = MOE_TILE
    n_tiles = rows // tm
    ne, _, n_up = w_up.shape
    de = w_down.shape[1]
    grid_spec = pltpu.PrefetchScalarGridSpec(
        num_scalar_prefetch=5,
        grid=(n_tiles,),
        in_specs=[
            pl.BlockSpec((tm, d), lambda i, te, nu, *_: (jnp.minimum(i, nu[0] - 1), 0)),
            pl.BlockSpec((1, 1, n_up), lambda i, te, *_: (te[i], 0, 0)),
            pl.BlockSpec((1, 1, d), lambda i, te, *_: (te[i], 0, 0)),
            pl.BlockSpec(memory_space=pl.ANY),
            pl.BlockSpec(memory_space=pl.ANY),
        ],
        out_specs=pl.BlockSpec((tm, d), lambda i, *_: (i, 0)),
        scratch_shapes=[
            pltpu.VMEM((2, d, n_up), F32),
            pltpu.VMEM((2, de, d), F32),
            pltpu.VMEM((d, n_up), BF16),
            pltpu.VMEM((de, d), BF16),
            pltpu.SemaphoreType.DMA((2, 2)),
        ],
    )
    return pl.pallas_call(
        _moe_kernel,
        grid_spec=grid_spec,
        out_shape=jax.ShapeDtypeStruct((rows, d), BF16),
        compiler_params=pltpu.CompilerParams(dimension_semantics=("arbitrary",), vmem_limit_bytes=MOE_VMEM_LIMIT),
        name="moe_experts",
    )(tile_expert, n_used, first, nxt, slot, xs, b_up.reshape(ne, 1, n_up), b_down.reshape(ne, 1, d), w_up, w_down)


def _combine_kernel(x_ref, y0_ref, y1_ref, y2_ref, y3_ref, gate_ref, mod_ref, o_ref):
    gates = gate_ref[0]
    acc = gates[:, 0:1] * y0_ref[...].astype(F32)
    for k, y_ref in ((1, y1_ref), (2, y2_ref), (3, y3_ref)):
        acc = acc + gates[:, k:k + 1] * y_ref[...].astype(F32)
    o_ref[0] = x_ref[0] + mod_ref[0, 0][5:6] * acc


def _moe_combine(x, yg, gates, mod, *, n_ctx):
    bsz, lt, d = x.shape
    tm = ROW_TILE
    nct = n_ctx // tm
    nt = lt // tm
    row = lambda i, t: (i, t, 0)
    slot = lambda k: pl.BlockSpec((tm, d), lambda i, t: (k * bsz * nt + i * nt + t, 0))
    return pl.pallas_call(
        _combine_kernel,
        grid=(bsz, nt),
        in_specs=[
            pl.BlockSpec((1, tm, d), row),
            slot(0), slot(1), slot(2), slot(3),
            pl.BlockSpec((1, tm, LANES), row),
            pl.BlockSpec((1, 1, 6, d), lambda i, t: (i, (t >= nct).astype(jnp.int32), 0, 0)),
        ],
        out_specs=pl.BlockSpec((1, tm, d), row),
        out_shape=jax.ShapeDtypeStruct((bsz, lt, d), F32),
        compiler_params=_params(("parallel", "parallel")),
        name="moe_combine",
    )(x, yg, yg, yg, yg, gates, mod)


def _moe_rows(n):
    return (-(-(n * TOP_K) // MOE_TILE) + N_EXPERTS) * MOE_TILE


def _moe_layer(x_mid, h2, idx, gates, mod, w_up, b_up, w_down, b_down, xs_buf, *, n_ctx):
    bsz, lt, d = x_mid.shape
    n = bsz * lt
    tm = MOE_TILE
    n_tiles = xs_buf.shape[0] // tm
    idx2 = idx.reshape(n, LANES)
    rank, cnt = _route_rank(idx2)
    counts = cnt[0, :N_EXPERTS].astype(jnp.int32)
    padded = (counts + tm - 1) // tm * tm
    pend = jnp.cumsum(padded)
    pstart = pend - padded
    experts = idx2[:, :TOP_K]
    onehot = experts[:, :, None] == jnp.arange(N_EXPERTS, dtype=jnp.int32)[None, None, :]
    dest = jnp.sum(jnp.where(onehot, pstart[None, None, :], 0), axis=-1) + rank[:, :TOP_K]
    n_used = (pend[-1] // tm).astype(jnp.int32)
    tile_start = jnp.arange(n_tiles, dtype=jnp.int32) * tm
    tile_e = jnp.sum((tile_start[:, None] >= pend[None, :]).astype(jnp.int32), axis=1)
    last_e = jnp.sum((jnp.maximum(n_used - 1, 0) * tm >= pend).astype(jnp.int32))
    tile_e = jnp.minimum(jnp.where(jnp.arange(n_tiles) < n_used, tile_e, last_e), N_EXPERTS - 1)
    prev_e = jnp.concatenate([jnp.full((1,), -1, jnp.int32), tile_e[:-1]])
    first = (tile_e != prev_e).astype(jnp.int32)
    slot = (jnp.cumsum(first) - 1) % 2
    ids = jnp.arange(N_EXPERTS, dtype=jnp.int32)
    later_used = (ids[None, :] > ids[:, None]) & (counts[None, :] > 0)
    next_used = jnp.min(jnp.where(later_used, ids[None, :], N_EXPERTS), axis=1)
    next_used = jnp.where(next_used == N_EXPERTS, -1, next_used)
    nxt = jnp.sum(jnp.where(tile_e[:, None] == ids[None, :], next_used[None, :], 0), axis=1)
    xs = _dispatch(h2.reshape(n, d), dest, xs_buf)
    ys = _moe_experts(xs, tile_e, n_used.reshape(1), first, nxt, slot, w_up, b_up, w_down, b_down)
    yg = ys.at[dest.T.reshape(-1)].get(mode='promise_in_bounds')
    return _moe_combine(x_mid, yg, gates, mod, n_ctx=n_ctx), xs


def _gdn_conv_kernel(p_ref, w_ref, o_ref, *, n_ctx):
    c = pl.program_id(1)
    lt, tc = p_ref.shape[1], p_ref.shape[2]
    x = p_ref[0].astype(F32)
    row = lax.broadcasted_iota(jnp.int32, (lt, 1), 0)
    is_ctx = row < n_ctx
    pos = jnp.where(is_ctx, row, row - n_ctx)
    seg_len = jnp.where(is_ctx, n_ctx, lt - n_ctx)
    acc = x * w_ref[CONV_K // 2:CONV_K // 2 + 1, :]
    for j in range(CONV_K):
        dlt = j - CONV_K // 2
        if dlt == 0:
            continue
        shifted = pltpu.roll(x, (-dlt) % lt, 0)
        valid = (pos + dlt >= 0) & (pos + dlt < seg_len)
        acc = acc + jnp.where(valid, shifted, 0.0) * w_ref[j:j + 1, :]
    y = acc * jax.nn.sigmoid(acc)
    n_qk_tiles = 2 * GDN_HEADS * GDN_DK // tc
    is_qk = c < n_qk_tiles
    scale = jnp.where(c < n_qk_tiles // 2, GDN_DK ** -0.5, 1.0)
    for hd in range(tc // LANES):
        t = y[:, hd * LANES:(hd + 1) * LANES]
        normed = t * lax.rsqrt(jnp.sum(t * t, axis=-1, keepdims=True) + NORM_EPS) * scale
        o_ref[0, :, hd * LANES:(hd + 1) * LANES] = jnp.where(is_qk, normed, t).astype(o_ref.dtype)


def _gdn_conv(p_main, conv_w, *, n_ctx):
    bsz, lt, _ = p_main.shape
    n = conv_w.shape[1]
    tc = 4 * LANES
    return pl.pallas_call(
        functools.partial(_gdn_conv_kernel, n_ctx=n_ctx),
        grid=(bsz, n // tc),
        in_specs=[
            pl.BlockSpec((1, lt, tc), lambda i, c: (i, 0, c)),
            pl.BlockSpec((CONV_K, tc), lambda i, c: (0, c)),
        ],
        out_specs=pl.BlockSpec((1, lt, tc), lambda i, c: (i, 0, c)),
        out_shape=jax.ShapeDtypeStruct((bsz, lt, n), BF16),
        compiler_params=_params(("parallel", "parallel")),
        name="gdn_conv",
    )(p_main, conv_w)


def _gdn_delta_kernel(qf_ref, kf_ref, vf_ref, qb_ref, kb_ref, vb_ref, colf_ref, colb_ref, rowf_ref, rowb_ref,
                      of_ref, ob_ref, s_ref):
    step = pl.program_id(1)

    @pl.when(step == 0)
    def _():
        s_ref[...] = jnp.zeros_like(s_ref)

    ii = lax.broadcasted_iota(jnp.int32, (CHUNK, CHUNK), 0)
    jj = lax.broadcasted_iota(jnp.int32, (CHUNK, CHUNK), 1)
    eye = (ii == jj).astype(F32)
    dirs = (
        (0, qf_ref, kf_ref, vf_ref, colf_ref, rowf_ref, of_ref, ii >= jj, ii > jj, CHUNK - 1),
        (1, qb_ref, kb_ref, vb_ref, colb_ref, rowb_ref, ob_ref, ii <= jj, ii < jj, 0),
    )
    chains = []
    for d, q_ref, k_ref, v_ref, col_ref, row_ref, o_ref, tri, strict, last in dirs:
        col = col_ref[0, 0]
        rows = row_ref[0, 0]
        for hd in range(GDN_HEADS):
            gi = d * GDN_HEADS + hd
            chains.append(dict(
                gi=gi, cols=slice(hd * LANES, (hd + 1) * LANES), q_ref=q_ref, k_ref=k_ref, v_ref=v_ref, o_ref=o_ref,
                tri=tri, strict=strict, last=last,
                gc=col[:, gi:gi + 1], beta=col[:, 2 * GDN_HEADS + gi:2 * GDN_HEADS + gi + 1], gr=rows[gi:gi + 1, :]))

    for c in chains:
        k = c['k_ref'][0, :, c['cols']]
        c['k16'] = k
        c['kbeta'] = k.astype(F32) * c['beta']
        c['decay'] = jnp.where(c['tri'], jnp.exp(jnp.where(c['tri'], c['gc'] - c['gr'], 0.0)), 0.0)
    for c in chains:
        c['p'] = jnp.where(c['strict'], _dot_nt(c['kbeta'].astype(BF16), c['k16']) * c['decay'], 0.0)
        c['x'] = eye - c['p']
    for _ in range((CHUNK - 1).bit_length() - 1):
        for c in chains:
            p16 = c['p'].astype(BF16)
            c['p'] = _dot(p16, p16)
        for c in chains:
            c['x'] = c['x'] + _dot(c['x'].astype(BF16), c['p'].astype(BF16))
    for c in chains:
        t_inv = c['x'].astype(BF16)
        v = c['v_ref'][0, :, c['cols']].astype(F32)
        c['u'] = _dot(t_inv, (v * c['beta']).astype(BF16))
        c['w'] = _dot(t_inv, (c['kbeta'] * jnp.exp(c['gc'])).astype(BF16))
    for c in chains:
        q = c['q_ref'][0, :, c['cols']]
        c['intra'] = (_dot_nt(q, c['k16']) * c['decay']).astype(BF16)
        c['qg'] = (q.astype(F32) * jnp.exp(c['gc'])).astype(BF16)
    for c in chains:
        c['s'] = s_ref[c['gi']]
        c['s16'] = c['s'].astype(BF16)
        c['v_new'] = (c['u'] - _dot(c['w'].astype(BF16), c['s16'])).astype(BF16)
    for c in chains:
        c['o_ref'][0, :, c['cols']] = _dot(c['qg'], c['s16']) + _dot(c['intra'], c['v_new'])
    for c in chains:
        g_last = c['gc'][c['last']:c['last'] + 1, :]
        k_dec = (c['k16'].astype(F32) * jnp.exp(g_last - c['gc'])).astype(BF16)
        s_ref[c['gi']] = c['s'] * jnp.exp(g_last) + _dot_tn(k_dec, c['v_new'])


def _gdn_delta(qkv, col_pack, row_pack, *, n_ctx):
    bsz, lt, _ = qkv.shape
    nc = lt // CHUNK
    ncc = n_ctx // CHUNK
    kw = GDN_HEADS * GDN_DK

    def bwd_chunk(c):
        return jnp.where(c < ncc, ncc - 1 - c, nc - 1 - (c - ncc))

    fwd = lambda blk: pl.BlockSpec((1, CHUNK, kw), lambda i, c: (i, c, blk))
    bwd = lambda blk: pl.BlockSpec((1, CHUNK, kw), lambda i, c: (i, bwd_chunk(c), blk))
    ncol, nrow = col_pack.shape[3], row_pack.shape[2]
    return pl.pallas_call(
        _gdn_delta_kernel,
        grid=(bsz, nc),
        in_specs=[
            fwd(0), fwd(1), fwd(2), bwd(0), bwd(1), bwd(2),
            pl.BlockSpec((1, 1, CHUNK, ncol), lambda i, c: (i, c, 0, 0)),
            pl.BlockSpec((1, 1, CHUNK, ncol), lambda i, c: (i, bwd_chunk(c), 0, 0)),
            pl.BlockSpec((1, 1, nrow, CHUNK), lambda i, c: (i, c, 0, 0)),
            pl.BlockSpec((1, 1, nrow, CHUNK), lambda i, c: (i, bwd_chunk(c), 0, 0)),
        ],
        out_specs=[
            pl.BlockSpec((1, CHUNK, kw), lambda i, c: (i, c, 0)),
            pl.BlockSpec((1, CHUNK, kw), lambda i, c: (i, bwd_chunk(c), 0)),
        ],
        out_shape=[jax.ShapeDtypeStruct((bsz, lt, kw), F32), jax.ShapeDtypeStruct((bsz, lt, kw), F32)],
        scratch_shapes=[pltpu.VMEM((2 * GDN_HEADS, GDN_DK, GDN_DK), F32)],
        compiler_params=_params(("parallel", "arbitrary")),
        name="gdn_delta",
    )(qkv, qkv, qkv, qkv, qkv, qkv, col_pack, col_pack, row_pack, row_pack)


def _gdn_gates(p_small, a_log, dt_bias):
    bsz, lt, _ = p_small.shape
    nh = GDN_HEADS
    beta = jax.nn.sigmoid(p_small[..., :2 * nh])
    a = p_small[..., 2 * nh:4 * nh]
    g = -jnp.exp(a_log.reshape(2 * nh)) * jax.nn.softplus(a + dt_bias.reshape(2 * nh))
    gch = g.reshape(bsz, lt // CHUNK, CHUNK, 2 * nh)
    gf = jnp.cumsum(gch[..., :nh], axis=2)
    gb = jnp.flip(jnp.cumsum(jnp.flip(gch[..., nh:], axis=2), axis=2), axis=2)
    gc = jnp.concatenate([gf, gb], axis=-1)
    col_pack = jnp.concatenate([gc, beta.reshape(bsz, lt // CHUNK, CHUNK, 2 * nh)], axis=-1)
    row_pack = jnp.swapaxes(gc, 2, 3)
    return col_pack, row_pack


def _final_norm_kernel(x_ref, g_ref, o_ref):
    o_ref[0] = _rms(x_ref[0]) * g_ref[...]


def _final_norm(x, g, *, n_ctx):
    bsz, lt, d = x.shape
    tm = ROW_TILE
    nct = n_ctx // tm
    return pl.pallas_call(
        _final_norm_kernel,
        grid=(bsz, (lt - n_ctx) // tm),
        in_specs=[pl.BlockSpec((1, tm, d), lambda i, t: (i, t + nct, 0)), pl.BlockSpec((1, d), lambda i, t: (0, 0))],
        out_specs=pl.BlockSpec((1, tm, d), lambda i, t: (i, t, 0)),
        out_shape=jax.ShapeDtypeStruct((bsz, lt - n_ctx, d), F32),
        compiler_params=_params(("parallel", "parallel")),
        name="final_norm",
    )(x, g)


def _rope_tables(n_ctx, seq):
    half = AXIS_DIM // 2
    lane = jnp.arange(LANES)
    inv = ROPE_THETA ** (-jnp.arange(0, AXIS_DIM, 2, dtype=F32) / AXIS_DIM)
    freq = inv[lane % half]
    use_col = (lane % HEAD_DIM) // AXIS_DIM == 1
    sign = jnp.where((lane // half) % 2 == 0, -1.0, 1.0)
    tok = jnp.arange(seq)
    pos = jnp.where(use_col[None, :], (tok % GRID_W)[:, None], (tok // GRID_W)[:, None]).astype(F32)
    ang = pos * freq[None, :]
    cos = jnp.concatenate([jnp.ones((n_ctx, LANES), F32), jnp.cos(ang)], axis=0)
    sin = jnp.concatenate([jnp.zeros((n_ctx, LANES), F32), jnp.sin(ang) * sign[None, :]], axis=0)
    return cos, sin


def _pad_cols(a, n):
    return jnp.pad(a, [(0, 0)] * (a.ndim - 1) + [(0, n - a.shape[-1])])


def kernel(x, c, ctx, c_ctx, w_mod, b_mod, g_mix, g_ffn, win_w_qkv, win_b_qkv, win_sink, win_w_o, win_b_o, glb_w_qkv, glb_g_q, glb_g_k, glb_w_o, gdn_w_in, gdn_conv_w, gdn_a_log, gdn_dt_bias, gdn_g_out, gdn_w_o, moe_w_router, moe_b_router, moe_w_up, moe_b_up, moe_w_down, moe_b_down, g_final):
    bsz, seq, d = x.shape
    n_ctx = ctx.shape[1]
    depth = w_mod.shape[0]
    assert n_ctx % ROW_TILE == 0 and seq % ROW_TILE == 0 and n_ctx >= WINDOW and seq >= ATTN_TQ + 2 * WINDOW
    nq, nk = N_HEADS * HEAD_DIM, N_KV_HEADS * HEAD_DIM

    xs = jnp.concatenate([ctx, x], axis=1)
    cond_rows = -(-(bsz + 1) // 8) * 8
    cond = jnp.zeros((cond_rows, d), F32).at[:bsz].set(c).at[bsz].set(c_ctx)
    mods = _adaln(cond, w_mod, b_mod).reshape(depth, cond_rows, 6, d)
    cos, sin = _rope_tables(n_ctx, seq)
    zero_bias = jnp.zeros((1, d), F32)
    moe_buf = jnp.zeros((_moe_rows(bsz * (n_ctx + seq)), d), F32)

    for i in range(depth):
        kind, j = i % 3, i // 3
        mod = jnp.stack([jnp.broadcast_to(mods[i, bsz], (bsz, 6, d)), mods[i, :bsz]], axis=1)
        g1 = g_mix[i].reshape(1, d)
        w_r = _pad_cols(moe_w_router[i], LANES)
        w_r_hi = w_r.astype(BF16)
        w_r = jnp.stack([w_r_hi, (w_r - w_r_hi.astype(F32)).astype(BF16)])
        b_r = jnp.concatenate([moe_b_router[i], jnp.full((LANES - N_EXPERTS,), NEG_INF, F32)]).reshape(1, LANES)
        route_args = (g_ffn[i].reshape(1, d), w_r, b_r)
        if kind == 0:
            qkv = _qkv_project(xs, mod, g1, win_w_qkv[j].astype(BF16), win_b_qkv[j].reshape(1, -1), cos, sin,
                               jnp.ones((1, nq + nk), F32), n_ctx=n_ctx, qk_norm=False)
            o = _attention(qkv, win_sink[j], windowed=True, n_ctx=n_ctx)
            outs = _mixer_out((o,), xs, mod, win_w_o[j].astype(BF16), win_b_o[j].reshape(1, d), *route_args,
                              n_ctx=n_ctx)
        elif kind == 1:
            hg = jnp.concatenate([jnp.tile(glb_g_q[j], N_HEADS), jnp.tile(glb_g_k[j], N_KV_HEADS)]).reshape(1, -1)
            qkv = _qkv_project(xs, mod, g1, glb_w_qkv[j].astype(BF16), jnp.zeros((1, nq + 2 * nk), F32), cos, sin,
                               hg, n_ctx=n_ctx, qk_norm=True)
            o = _attention(qkv, jnp.zeros((N_HEADS,), F32), windowed=False, n_ctx=n_ctx)
            outs = _mixer_out((o,), xs, mod, glb_w_o[j].astype(BF16), zero_bias, *route_args, n_ctx=n_ctx)
        else:
            n_main = 2 * GDN_HEADS * GDN_DK + 2 * GDN_HEADS * LANES
            w_in = _pad_cols(gdn_w_in[j], n_main + LANES).astype(BF16)
            p_main, p_small = _gdn_in_project(xs, mod, g1, w_in, n_ctx=n_ctx, n_main=n_main)
            qkv = _gdn_conv(p_main, gdn_conv_w[j], n_ctx=n_ctx)
            col_pack, row_pack = _gdn_gates(p_small, gdn_a_log[j], gdn_dt_bias[j])
            o_f, o_b = _gdn_delta(qkv, col_pack, row_pack, n_ctx=n_ctx)
            outs = _mixer_out((o_f, o_b, p_main), xs, mod, gdn_w_o[j].astype(BF16), zero_bias, *route_args,
                              n_ctx=n_ctx, gdn_gain=gdn_g_out[j].reshape(1, LANES))
        x_mid, h2, idx, gates = outs
        xs, moe_buf = _moe_layer(x_mid, h2, idx, gates, mod, moe_w_up[i], moe_b_up[i], moe_w_down[i], moe_b_down[i],
                                 moe_buf, n_ctx=n_ctx)
    return _final_norm(xs, g_final.reshape(1, d), n_ctx=n_ctx)
```

```python
import functools

import jax
import jax.numpy as jnp
from jax import lax
from jax.experimental import pallas as pl
from jax.experimental.pallas import tpu as pltpu

F32 = jnp.float32
BF16 = jnp.bfloat16
HIGHEST = lax.Precision.HIGHEST

GRID_W = 64
HEAD_DIM = 64
N_HEADS = 16
N_KV_HEADS = 4
WINDOW = 128
ROPE_THETA = 10000.0
AXIS_DIM = HEAD_DIM // 2
GDN_HEADS = 8
GDN_DK = 128
CONV_K = 5
CHUNK = 64
N_EXPERTS = 32
TOP_K = 4
SWIGLU_LIMIT = 7.0
SWIGLU_ALPHA = 1.702
NORM_EPS = 1e-6
NEG_INF = -1e30

LANES = 128
ROW_TILE = 256
ATTN_TQ = 128
MIX_SUB = 128
MOE_TILE = 256
VMEM_LIMIT = 48 * 1024 * 1024
MOE_VMEM_LIMIT = 56 * 1024 * 1024


def _params(sem):
    return pltpu.CompilerParams(dimension_semantics=sem, vmem_limit_bytes=VMEM_LIMIT)


def _rms(x, eps=NORM_EPS):
    return x * lax.rsqrt(jnp.mean(x * x, axis=-1, keepdims=True) + eps)


def _dot(a, b):
    return jnp.dot(a, b, preferred_element_type=F32)


def _dot_nt(a, b):
    return lax.dot_general(a, b, (((1,), (1,)), ((), ())), preferred_element_type=F32)


def _dot_tn(a, b):
    return lax.dot_general(a, b, (((0,), (0,)), ((), ())), preferred_element_type=F32)


def _adaln_kernel(cond_ref, w_ref, b_ref, o_ref):
    c = cond_ref[...]
    a = (c * jax.nn.sigmoid(c)).astype(BF16)
    o_ref[0] = _dot(a, w_ref[0].astype(BF16)) + b_ref[0]


def _adaln(cond, w_mod, b_mod):
    depth, d, n = w_mod.shape
    r = cond.shape[0]
    tn = n // 4
    return pl.pallas_call(
        _adaln_kernel,
        grid=(depth, n // tn),
        in_specs=[
            pl.BlockSpec((r, d), lambda i, j: (0, 0)),
            pl.BlockSpec((1, d, tn), lambda i, j: (i, 0, j)),
            pl.BlockSpec((1, 1, tn), lambda i, j: (i, 0, j)),
        ],
        out_specs=pl.BlockSpec((1, r, tn), lambda i, j: (i, 0, j)),
        out_shape=jax.ShapeDtypeStruct((depth, r, n), F32),
        compiler_params=_params(("parallel", "parallel")),
        name="adaln",
    )(cond, w_mod, b_mod.reshape(depth, 1, n))


def _norm_mod(x, m, g, which):
    return _rms(x) * g * (1.0 + m[which + 1:which + 2]) + m[which:which + 1]


def _swap16(t):
    lane = lax.broadcasted_iota(jnp.int32, t.shape, 1)
    first = (lane // (AXIS_DIM // 2)) % 2 == 0
    return jnp.where(first, pltpu.roll(t, LANES - AXIS_DIM // 2, 1), pltpu.roll(t, AXIS_DIM // 2, 1))


def _head_sum_matrix():
    r = lax.broadcasted_iota(jnp.int32, (LANES, LANES), 0) // HEAD_DIM
    c = lax.broadcasted_iota(jnp.int32, (LANES, LANES), 1) // HEAD_DIM
    return (r == c).astype(F32)


def _qkv_kernel(x_ref, mod_ref, g_ref, w_ref, b_ref, cos_ref, sin_ref, hg_ref, o_ref, *, qk_norm):
    nq = N_HEADS * HEAD_DIM
    nk = N_KV_HEADS * HEAD_DIM
    h = _norm_mod(x_ref[0], mod_ref[0, 0], g_ref[...], 0)
    p = _dot(h.astype(BF16), w_ref[...]) + b_ref[...]
    cos = cos_ref[...]
    sin = sin_ref[...]
    lane = lax.broadcasted_iota(jnp.int32, cos.shape, 1)
    low = lane < HEAD_DIM
    if qk_norm:
        seg = _head_sum_matrix()

    def expand(t, base, ones_lane):
        r = pltpu.roll(t, HEAD_DIM, 1)
        fill_hi = jnp.where(lane == HEAD_DIM, 1.0, 0.0) if ones_lane else jnp.zeros_like(t)
        fill_lo = jnp.where(lane == 0, 1.0, 0.0) if ones_lane else jnp.zeros_like(t)
        parts = (jnp.where(low, t, fill_hi), jnp.where(low, fill_lo, r),
                 jnp.where(low, r, fill_hi), jnp.where(low, fill_lo, t))
        for i, part in enumerate(parts):
            o_ref[0, :, base + i * LANES: base + (i + 1) * LANES] = part.astype(o_ref.dtype)

    for j in range((nq + nk) // LANES):
        t = p[:, j * LANES:(j + 1) * LANES]
        if qk_norm:
            ss = jnp.dot(t * t, seg, precision=HIGHEST, preferred_element_type=F32)
            t = t * lax.rsqrt(ss * (1.0 / HEAD_DIM) + NORM_EPS) * hg_ref[:, j * LANES:(j + 1) * LANES]
        t = t * cos + _swap16(t) * sin
        if j < nq // LANES:
            o_ref[0, :, j * LANES:(j + 1) * LANES] = (t * (HEAD_DIM ** -0.5)).astype(o_ref.dtype)
        else:
            expand(t, nq + (j - nq // LANES) * 4 * LANES, False)
    for j in range(nk // LANES):
        t = p[:, nq + nk + j * LANES: nq + nk + (j + 1) * LANES]
        expand(t, nq + 4 * nk + j * 4 * LANES, True)


def _qkv_project(x, mod, g, w, b, cos, sin, hg, *, n_ctx, qk_norm):
    bsz, lt, d = x.shape
    n = w.shape[1]
    tm = ROW_TILE
    nct = n_ctx // tm
    n_out = N_HEADS * HEAD_DIM + 8 * N_KV_HEADS * HEAD_DIM
    return pl.pallas_call(
        functools.partial(_qkv_kernel, qk_norm=qk_norm),
        grid=(bsz, lt // tm),
        in_specs=[
            pl.BlockSpec((1, tm, d), lambda i, t: (i, t, 0)),
            pl.BlockSpec((1, 1, 6, d), lambda i, t: (i, (t >= nct).astype(jnp.int32), 0, 0)),
            pl.BlockSpec((1, d), lambda i, t: (0, 0)),
            pl.BlockSpec((d, n), lambda i, t: (0, 0)),
            pl.BlockSpec((1, n), lambda i, t: (0, 0)),
            pl.BlockSpec((tm, LANES), lambda i, t: (t, 0)),
            pl.BlockSpec((tm, LANES), lambda i, t: (t, 0)),
            pl.BlockSpec((1, hg.shape[1]), lambda i, t: (0, 0)),
        ],
        out_specs=pl.BlockSpec((1, tm, n_out), lambda i, t: (i, t, 0)),
        out_shape=jax.ShapeDtypeStruct((bsz, lt, n_out), BF16),
        compiler_params=_params(("parallel", "parallel")),
        name="qkv_project",
    )(x, mod, g, w, b, cos, sin, hg)


def _gdn_in_kernel(x_ref, mod_ref, g_ref, w_ref, o_ref, o2_ref):
    h = _norm_mod(x_ref[0], mod_ref[0, 0], g_ref[...], 0)
    p = _dot(h.astype(BF16), w_ref[...])
    n_main = o_ref.shape[2]
    o_ref[0] = p[:, :n_main].astype(o_ref.dtype)
    o2_ref[0] = p[:, n_main:]


def _gdn_in_project(x, mod, g, w, *, n_ctx, n_main):
    bsz, lt, d = x.shape
    n = w.shape[1]
    tm = ROW_TILE
    nct = n_ctx // tm
    return pl.pallas_call(
        _gdn_in_kernel,
        grid=(bsz, lt // tm),
        in_specs=[
            pl.BlockSpec((1, tm, d), lambda i, t: (i, t, 0)),
            pl.BlockSpec((1, 1, 6, d), lambda i, t: (i, (t >= nct).astype(jnp.int32), 0, 0)),
            pl.BlockSpec((1, d), lambda i, t: (0, 0)),
            pl.BlockSpec((d, n), lambda i, t: (0, 0)),
        ],
        out_specs=[
            pl.BlockSpec((1, tm, n_main), lambda i, t: (i, t, 0)),
            pl.BlockSpec((1, tm, n - n_main), lambda i, t: (i, t, 0)),
        ],
        out_shape=[
            jax.ShapeDtypeStruct((bsz, lt, n_main), BF16),
            jax.ShapeDtypeStruct((bsz, lt, n - n_main), F32),
        ],
        compiler_params=_params(("parallel", "parallel")),
        name="gdn_in_project",
    )(x, mod, g, w)


def _attn_kernel(sink_ref, q_ref, k_ref, v_ref, o_ref, *, windowed, n_ctx, seq):
    tq = q_ref.shape[1]
    t = pl.program_id(1)
    nct = n_ctx // tq
    gw = 2 * LANES
    top = lax.broadcasted_iota(jnp.int32, (2 * tq, 1), 0) < tq
    low = lax.broadcasted_iota(jnp.int32, (2 * tq, LANES), 1) < HEAD_DIM

    def run(segments, batch):
        for g0 in range(0, N_KV_HEADS, batch):
            heads = range(g0, g0 + batch)
            jobs = [(g, half) for g in heads for half in range(2)]
            qs = {g: jnp.concatenate([q_ref[0, :, g * gw:g * gw + LANES],
                                      q_ref[0, :, g * gw + LANES:(g + 1) * gw]], axis=0) for g in heads}
            scores, ms, outs = {}, {}, {}
            for g, half in jobs:
                cols = slice(g * gw + half * LANES, g * gw + (half + 1) * LANES)
                scores[g, half] = []
                for rows, mask in segments:
                    s = _dot_nt(qs[g], k_ref[0, rows, cols])
                    if mask is not None:
                        s = jnp.where(mask, s, NEG_INF)
                    scores[g, half].append(s)
            for g, half in jobs:
                m = scores[g, half][0].max(axis=-1, keepdims=True)
                for s in scores[g, half][1:]:
                    m = jnp.maximum(m, s.max(axis=-1, keepdims=True))
                if windowed:
                    m = jnp.maximum(m, jnp.where(top, sink_ref[g * 4 + half], sink_ref[g * 4 + 2 + half]))
                ms[g, half] = m
            for g, half in jobs:
                cols = slice(g * gw + half * LANES, g * gw + (half + 1) * LANES)
                o = None
                for s, (rows, _) in zip(scores[g, half], segments):
                    p = jnp.exp((s - ms[g, half]).astype(BF16))
                    pv = _dot(p, v_ref[0, rows, cols])
                    o = pv if o is None else o + pv
                outs[g, half] = o
            for g in heads:
                den_lo = outs[g, 0][:, HEAD_DIM:HEAD_DIM + 1]
                den_hi = outs[g, 1][:, 0:1]
                if windowed:
                    den_lo = den_lo + jnp.exp(jnp.where(top, sink_ref[g * 4], sink_ref[g * 4 + 2]) - ms[g, 0])
                    den_hi = den_hi + jnp.exp(jnp.where(top, sink_ref[g * 4 + 1], sink_ref[g * 4 + 3]) - ms[g, 1])
                acc = jnp.where(low, outs[g, 0] * (1.0 / den_lo), outs[g, 1] * (1.0 / den_hi))
                o_ref[0, :, g * gw:g * gw + LANES] = acc[:tq].astype(o_ref.dtype)
                o_ref[0, :, g * gw + LANES:(g + 1) * gw] = acc[tq:].astype(o_ref.dtype)

    ctx_rows = slice(0, n_ctx)

    @pl.when(t < nct)
    def _():
        run([(ctx_rows, None)], N_KV_HEADS)

    @pl.when(t >= nct)
    def _():
        if windowed:
            span = tq + 2 * WINDOW
            start = (t - nct) * tq
            r0 = pl.multiple_of(jnp.minimum(n_ctx + start - WINDOW, n_ctx + seq - span), LANES)
            qpos = start + lax.broadcasted_iota(jnp.int32, (tq, span), 0)
            kpos = r0 - n_ctx + lax.broadcasted_iota(jnp.int32, (tq, span), 1)
            band = (jnp.abs(qpos - kpos) <= WINDOW) & (kpos >= 0)
            band = jnp.concatenate([band, band], axis=0)
            run([(pl.ds(r0, span), band), (ctx_rows, None)], N_KV_HEADS)
        else:
            run([(slice(0, n_ctx + seq), None)], 1)


def _attention(qkv, sink, *, windowed, n_ctx):
    bsz, lt, _ = qkv.shape
    tq = ATTN_TQ
    nq = N_HEADS * HEAD_DIM
    kernel = functools.partial(_attn_kernel, windowed=windowed, n_ctx=n_ctx, seq=lt - n_ctx)
    return pl.pallas_call(
        kernel,
        grid=(bsz, lt // tq),
        in_specs=[
            pl.BlockSpec(memory_space=pltpu.SMEM),
            pl.BlockSpec((1, tq, nq), lambda i, t: (i, t, 0)),
            pl.BlockSpec((1, lt, nq), lambda i, t: (i, 0, 1)),
            pl.BlockSpec((1, lt, nq), lambda i, t: (i, 0, 2)),
        ],
        out_specs=pl.BlockSpec((1, tq, nq), lambda i, t: (i, t, 0)),
        out_shape=jax.ShapeDtypeStruct((bsz, lt, nq), BF16),
        compiler_params=_params(("parallel", "arbitrary")),
        name="window_attention" if windowed else "global_attention",
    )(sink, qkv, qkv, qkv)


def _sub_tiles(tm):
    return [slice(r, r + MIX_SUB) for r in range(0, tm, MIX_SUB)]


def _mix_out_tail(ys, x_ref, m, gffn_ref, wr_ref, br_ref, xmid_ref, h2_ref, idx_ref, gate_ref):
    g = gffn_ref[...]
    w_hi, w_lo, b_r = wr_ref[0], wr_ref[1], br_ref[...]
    hs = []
    for rows, y in ys:
        xm = x_ref[0, rows, :] + m[2:3] * y
        xmid_ref[0, rows, :] = xm
        h = _norm_mod(xm, m, g, 3)
        h2_ref[0, rows, :] = h
        hs.append(h)
    logits = []
    for h in hs:
        h_hi = h.astype(BF16)
        h_lo = (h - h_hi.astype(F32)).astype(BF16)
        logits.append(_dot(h_hi, w_hi) + (_dot(h_hi, w_lo) + _dot(h_lo, w_hi)) + b_r)
    lane = lax.broadcasted_iota(jnp.int32, logits[0].shape, 1)
    vals = [[] for _ in ys]
    idxs = [[] for _ in ys]
    for _ in range(TOP_K):
        for s in range(len(ys)):
            mx = logits[s].max(axis=-1, keepdims=True)
            ix = jnp.where(logits[s] == mx, lane, LANES).min(axis=-1, keepdims=True)
            vals[s].append(mx)
            idxs[s].append(ix)
            logits[s] = jnp.where(lane == ix, -jnp.inf, logits[s])
    for s, (rows, _) in enumerate(ys):
        es = [jnp.exp(v - vals[s][0]) for v in vals[s]]
        inv = 1.0 / (es[0] + es[1] + es[2] + es[3])
        gates = jnp.zeros(lane.shape, F32)
        idx = jnp.zeros(lane.shape, jnp.int32)
        for k in range(TOP_K):
            gates = jnp.where(lane == k, es[k] * inv, gates)
            idx = jnp.where(lane == k, idxs[s][k], idx)
        idx_ref[0, rows, :] = idx
        gate_ref[0, rows, :] = gates


def _attn_out_kernel(o_ref, x_ref, mod_ref, wo_ref, bo_ref, gffn_ref, wr_ref, br_ref,
                     xmid_ref, h2_ref, idx_ref, gate_ref):
    ys = [(rows, _dot(o_ref[0, rows, :], wo_ref[...]) + bo_ref[...]) for rows in _sub_tiles(o_ref.shape[1])]
    _mix_out_tail(ys, x_ref, mod_ref[0, 0], gffn_ref, wr_ref, br_ref, xmid_ref, h2_ref, idx_ref, gate_ref)


def _gdn_out_kernel(of_ref, ob_ref, z_ref, gout_ref, x_ref, mod_ref, wo_ref, bo_ref, gffn_ref, wr_ref, br_ref,
                    xmid_ref, h2_ref, idx_ref, gate_ref):
    gout = gout_ref[...]
    ys = []
    for rows in _sub_tiles(of_ref.shape[1]):
        parts = []
        for hd in range(GDN_HEADS):
            cols = slice(hd * LANES, (hd + 1) * LANES)
            z = z_ref[0, rows, cols].astype(F32)
            o = _rms(of_ref[0, rows, cols] + ob_ref[0, rows, cols]) * gout
            parts.append((o * (z * jax.nn.sigmoid(z))).astype(BF16))
        ys.append((rows, _dot(jnp.concatenate(parts, axis=1), wo_ref[...]) + bo_ref[...]))
    _mix_out_tail(ys, x_ref, mod_ref[0, 0], gffn_ref, wr_ref, br_ref, xmid_ref, h2_ref, idx_ref, gate_ref)


def _mixer_out(o_inputs, x, mod, w_o, b_o, g_ffn, w_r, b_r, *, n_ctx, gdn_gain=None):
    bsz, lt, d = x.shape
    tm = ROW_TILE
    nct = n_ctx // tm
    row = lambda i, t: (i, t, 0)
    const = lambda i, t: (0, 0)
    if gdn_gain is None:
        kernel = _attn_out_kernel
        head_specs = [pl.BlockSpec((1, tm, d), row)]
        head_args = list(o_inputs)
    else:
        kernel = _gdn_out_kernel
        o_f, o_b, p_main = o_inputs
        zblk = p_main.shape[2] // d - 1
        head_specs = [pl.BlockSpec((1, tm, d), row), pl.BlockSpec((1, tm, d), row),
                      pl.BlockSpec((1, tm, d), lambda i, t: (i, t, zblk)),
                      pl.BlockSpec((1, LANES), const)]
        head_args = [o_f, o_b, p_main, gdn_gain]
    return pl.pallas_call(
        kernel,
        grid=(bsz, lt // tm),
        in_specs=head_specs + [
            pl.BlockSpec((1, tm, d), row),
            pl.BlockSpec((1, 1, 6, d), lambda i, t: (i, (t >= nct).astype(jnp.int32), 0, 0)),
            pl.BlockSpec((d, d), const),
            pl.BlockSpec((1, d), const),
            pl.BlockSpec((1, d), const),
            pl.BlockSpec((2, d, LANES), lambda i, t: (0, 0, 0)),
            pl.BlockSpec((1, LANES), const),
        ],
        out_specs=[
            pl.BlockSpec((1, tm, d), row),
            pl.BlockSpec((1, tm, d), row),
            pl.BlockSpec((1, tm, LANES), row),
            pl.BlockSpec((1, tm, LANES), row),
        ],
        out_shape=[
            jax.ShapeDtypeStruct((bsz, lt, d), F32),
            jax.ShapeDtypeStruct((bsz, lt, d), F32),
            jax.ShapeDtypeStruct((bsz, lt, LANES), jnp.int32),
            jax.ShapeDtypeStruct((bsz, lt, LANES), F32),
        ],
        compiler_params=_params(("parallel", "parallel")),
        name="mixer_out_router",
    )(*head_args, x, mod, w_o, b_o, g_ffn, w_r, b_r)


def _route_rank_kernel(idx_ref, rank_ref, cnt_ref, run_ref):
    i = pl.program_id(0)

    @pl.when(i == 0)
    def _():
        run_ref[...] = jnp.zeros_like(run_ref)

    idx = idx_ref[...]
    tm = idx.shape[0]
    lane = lax.broadcasted_iota(jnp.int32, idx.shape, 1)
    earlier = (lax.broadcasted_iota(jnp.int32, (tm, tm), 1) < lax.broadcasted_iota(jnp.int32, (tm, tm), 0)).astype(BF16)
    base = run_ref[...]
    rank = jnp.zeros(idx.shape, jnp.int32)
    for k in range(TOP_K):
        hit = lane == idx[:, k:k + 1]
        before = _dot(earlier, hit.astype(BF16)) + base
        r = jnp.sum(jnp.where(hit, before, 0.0), axis=-1, keepdims=True)
        rank = jnp.where(lane == k, r.astype(jnp.int32), rank)
        base = base + jnp.sum(hit.astype(F32), axis=0, keepdims=True)
    rank_ref[...] = rank
    run_ref[...] = base
    cnt_ref[...] = jnp.broadcast_to(base, cnt_ref.shape)


def _route_rank(idx):
    n = idx.shape[0]
    tm = 2 * ROW_TILE if n % (2 * ROW_TILE) == 0 else ROW_TILE
    return pl.pallas_call(
        _route_rank_kernel,
        grid=(n // tm,),
        in_specs=[pl.BlockSpec((tm, LANES), lambda i: (i, 0))],
        out_specs=[pl.BlockSpec((tm, LANES), lambda i: (i, 0)), pl.BlockSpec((8, LANES), lambda i: (0, 0))],
        out_shape=[jax.ShapeDtypeStruct((n, LANES), jnp.int32), jax.ShapeDtypeStruct((8, LANES), F32)],
        scratch_shapes=[pltpu.VMEM((1, LANES), F32)],
        compiler_params=_params(("arbitrary",)),
        name="route_rank",
    )(idx)


def _dispatch_kernel(dest_ref, h_ref, xs_in_ref, xs_ref, sem):
    del xs_in_ref
    tm = h_ref.shape[0]

    def row_copy(r, dst):
        return pltpu.make_async_copy(h_ref.at[pl.ds(r, 1)], xs_ref.at[pl.ds(dst, 1)], sem)

    def issue(r, carry):
        for k in range(TOP_K):
            row_copy(r, dest_ref[0, 0, r * TOP_K + k]).start(priority=k % 2)
        return carry

    lax.fori_loop(0, tm, issue, 0, unroll=8)
    def drain(r, carry):
        for k in range(TOP_K):
            row_copy(r, 0).wait()
        return carry

    lax.fori_loop(0, tm, drain, 0, unroll=8)


def _dispatch(h2, dest, xs_buf):
    n, d = h2.shape
    rows = xs_buf.shape[0]
    tm = ROW_TILE
    dest3 = dest.reshape(n // tm, 1, tm * TOP_K)
    return pl.pallas_call(
        _dispatch_kernel,
        grid=(n // tm,),
        in_specs=[
            pl.BlockSpec((1, 1, tm * TOP_K), lambda i: (i, 0, 0), memory_space=pltpu.SMEM),
            pl.BlockSpec((tm, d), lambda i: (i, 0)),
            pl.BlockSpec(memory_space=pl.ANY),
        ],
        out_specs=pl.BlockSpec(memory_space=pl.ANY),
        out_shape=jax.ShapeDtypeStruct((rows, d), h2.dtype),
        scratch_shapes=[pltpu.SemaphoreType.DMA],
        input_output_aliases={2: 0},
        compiler_params=_params(("arbitrary",)),
        name="moe_dispatch",
    )(dest3, h2, xs_buf)


def _moe_kernel(te_ref, nu_ref, first_ref, nxt_ref, slot_ref, x_ref, bu_ref, bd_ref, wu_hbm, wd_hbm, o_ref,
                wu_f32, wd_f32, wu16, wd16, sem, *, layer):
    i = pl.program_id(0)
    de = wd16.shape[0]

    def weight_copies(expert, slot):
        return (pltpu.make_async_copy(wu_hbm.at[layer, expert], wu_f32.at[slot], sem.at[0, slot]),
                pltpu.make_async_copy(wd_hbm.at[layer, expert], wd_f32.at[slot], sem.at[1, slot]))

    active = i < nu_ref[0]

    @pl.when(active & (i == 0))
    def _():
        for cp in weight_copies(te_ref[0], 0):
            cp.start()

    @pl.when(active & (first_ref[i] == 1))
    def _():
        slot = slot_ref[i]
        for cp in weight_copies(te_ref[i], slot):
            cp.wait()

        @pl.when(nxt_ref[i] >= 0)
        def _():
            for cp in weight_copies(nxt_ref[i], 1 - slot):
                cp.start()

        wu16[...] = wu_f32[slot].astype(BF16)
        wd16[...] = wd_f32[slot].astype(BF16)

    @pl.when(active)
    def _():
        u = _dot(x_ref[...].astype(BF16), wu16[...]) + bu_ref[0]
        glu = jnp.minimum(u[:, :de], SWIGLU_LIMIT)
        lin = jnp.clip(u[:, de:], -SWIGLU_LIMIT, SWIGLU_LIMIT)
        act = glu * jax.nn.sigmoid(SWIGLU_ALPHA * glu) * (lin + 1.0)
        o_ref[...] = (_dot(act.astype(BF16), wd16[...]) + bd_ref[0]).astype(o_ref.dtype)

    @pl.when(jnp.logical_not(active))
    def _():
        o_ref[...] = jnp.zeros_like(o_ref)


def _moe_experts(xs, tile_expert, n_used, first, nxt, slot, w_up, b_up, w_down, b_down, *, layer):
    rows, d = xs.shape
    tm = MOE_TILE
    n_tiles = rows // tm
    _, ne, _, n_up = w_up.shape
    de = w_down.shape[2]
    grid_spec = pltpu.PrefetchScalarGridSpec(
        num_scalar_prefetch=5,
        grid=(n_tiles,),
        in_specs=[
            pl.BlockSpec((tm, d), lambda i, te, nu, *_: (jnp.minimum(i, nu[0] - 1), 0)),
            pl.BlockSpec((1, 1, n_up), lambda i, te, *_: (te[i], 0, 0)),
            pl.BlockSpec((1, 1, d), lambda i, te, *_: (te[i], 0, 0)),
            pl.BlockSpec(memory_space=pl.ANY),
            pl.BlockSpec(memory_space=pl.ANY),
        ],
        out_specs=pl.BlockSpec((tm, d), lambda i, *_: (i, 0)),
        scratch_shapes=[
            pltpu.VMEM((2, d, n_up), F32),
            pltpu.VMEM((2, de, d), F32),
            pltpu.VMEM((d, n_up), BF16),
            pltpu.VMEM((de, d), BF16),
            pltpu.SemaphoreType.DMA((2, 2)),
        ],
    )
    return pl.pallas_call(
        functools.partial(_moe_kernel, layer=layer),
        grid_spec=grid_spec,
        out_shape=jax.ShapeDtypeStruct((rows, d), BF16),
        compiler_params=pltpu.CompilerParams(dimension_semantics=("arbitrary",), vmem_limit_bytes=MOE_VMEM_LIMIT),
        name="moe_experts",
    )(tile_expert, n_used, first, nxt, slot, xs, b_up.reshape(ne, 1, n_up), b_down.reshape(ne, 1, d), w_up, w_down)


def _combine_kernel(x_ref, y0_ref, y1_ref, y2_ref, y3_ref, gate_ref, mod_ref, gfin_ref, o_ref, *, final):
    gates = gate_ref[0]
    acc = gates[:, 0:1] * y0_ref[...].astype(F32)
    for k, y_ref in ((1, y1_ref), (2, y2_ref), (3, y3_ref)):
        acc = acc + gates[:, k:k + 1] * y_ref[...].astype(F32)
    x_new = x_ref[0] + mod_ref[0, 0][5:6] * acc
    o_ref[0] = _rms(x_new) * gfin_ref[...] if final else x_new


def _moe_combine(x, yg, gates, mod, *, n_ctx, final_gain=None):
    bsz, lt, d = x.shape
    tm = ROW_TILE
    nct = n_ctx // tm
    nt = lt // tm
    final = final_gain is not None
    skip = nct if final else 0
    row = lambda i, t: (i, t + skip, 0)
    slot = lambda k: pl.BlockSpec((tm, d), lambda i, t: (k * bsz * nt + i * nt + t + skip, 0))
    return pl.pallas_call(
        functools.partial(_combine_kernel, final=final),
        grid=(bsz, nt - skip),
        in_specs=[
            pl.BlockSpec((1, tm, d), row),
            slot(0), slot(1), slot(2), slot(3),
            pl.BlockSpec((1, tm, LANES), row),
            pl.BlockSpec((1, 1, 6, d), lambda i, t: (i, (t + skip >= nct).astype(jnp.int32), 0, 0)),
            pl.BlockSpec((1, d), lambda i, t: (0, 0)),
        ],
        out_specs=pl.BlockSpec((1, tm, d), lambda i, t: (i, t, 0)),
        out_shape=jax.ShapeDtypeStruct((bsz, lt - skip * tm, d), F32),
        compiler_params=_params(("parallel", "parallel")),
        name="moe_combine",
    )(x, yg, yg, yg, yg, gates, mod, final_gain if final else jnp.ones((1, d), F32))


def _moe_rows(n):
    return (-(-(n * TOP_K) // MOE_TILE) + N_EXPERTS) * MOE_TILE


def _moe_layer(x_mid, h2, idx, gates, mod, w_up, b_up, w_down, b_down, xs_buf, *, n_ctx, layer, final_gain=None):
    bsz, lt, d = x_mid.shape
    n = bsz * lt
    tm = MOE_TILE
    n_tiles = xs_buf.shape[0] // tm
    idx2 = idx.reshape(n, LANES)
    rank, cnt = _route_rank(idx2)
    counts = cnt[0, :N_EXPERTS].astype(jnp.int32)
    padded = (counts + tm - 1) // tm * tm
    pend = jnp.cumsum(padded)
    pstart = pend - padded
    experts = idx2[:, :TOP_K]
    onehot = experts[:, :, None] == jnp.arange(N_EXPERTS, dtype=jnp.int32)[None, None, :]
    dest = jnp.sum(jnp.where(onehot, pstart[None, None, :], 0), axis=-1) + rank[:, :TOP_K]
    n_used = (pend[-1] // tm).astype(jnp.int32)
    tile_start = jnp.arange(n_tiles, dtype=jnp.int32) * tm
    tile_e = jnp.sum((tile_start[:, None] >= pend[None, :]).astype(jnp.int32), axis=1)
    last_e = jnp.sum((jnp.maximum(n_used - 1, 0) * tm >= pend).astype(jnp.int32))
    tile_e = jnp.minimum(jnp.where(jnp.arange(n_tiles) < n_used, tile_e, last_e), N_EXPERTS - 1)
    prev_e = jnp.concatenate([jnp.full((1,), -1, jnp.int32), tile_e[:-1]])
    first = (tile_e != prev_e).astype(jnp.int32)
    slot = (jnp.cumsum(first) - 1) % 2
    ids = jnp.arange(N_EXPERTS, dtype=jnp.int32)
    later_used = (ids[None, :] > ids[:, None]) & (counts[None, :] > 0)
    next_used = jnp.min(jnp.where(later_used, ids[None, :], N_EXPERTS), axis=1)
    next_used = jnp.where(next_used == N_EXPERTS, -1, next_used)
    nxt = jnp.sum(jnp.where(tile_e[:, None] == ids[None, :], next_used[None, :], 0), axis=1)
    xs = _dispatch(h2.reshape(n, d), dest, xs_buf)
    ys = _moe_experts(xs, tile_e, n_used.reshape(1), first, nxt, slot, w_up, b_up, w_down, b_down, layer=layer)
    yg = ys.at[dest.T.reshape(-1)].get(mode='promise_in_bounds')
    return _moe_combine(x_mid, yg, gates, mod, n_ctx=n_ctx, final_gain=final_gain), xs


def _gdn_conv_kernel(p_ref, w_ref, o_ref, *, n_ctx):
    c = pl.program_id(1)
    lt, tc = p_ref.shape[1], p_ref.shape[2]
    x = p_ref[0].astype(F32)
    row = lax.broadcasted_iota(jnp.int32, (lt, 1), 0)
    is_ctx = row < n_ctx
    pos = jnp.where(is_ctx, row, row - n_ctx)
    seg_len = jnp.where(is_ctx, n_ctx, lt - n_ctx)
    acc = x * w_ref[CONV_K // 2:CONV_K // 2 + 1, :]
    for j in range(CONV_K):
        dlt = j - CONV_K // 2
        if dlt == 0:
            continue
        shifted = pltpu.roll(x, (-dlt) % lt, 0)
        valid = (pos + dlt >= 0) & (pos + dlt < seg_len)
        acc = acc + jnp.where(valid, shifted, 0.0) * w_ref[j:j + 1, :]
    y = acc * jax.nn.sigmoid(acc)
    n_qk_tiles = 2 * GDN_HEADS * GDN_DK // tc
    is_qk = c < n_qk_tiles
    scale = jnp.where(c < n_qk_tiles // 2, GDN_DK ** -0.5, 1.0)
    for hd in range(tc // LANES):
        t = y[:, hd * LANES:(hd + 1) * LANES]
        normed = t * lax.rsqrt(jnp.sum(t * t, axis=-1, keepdims=True) + NORM_EPS) * scale
        o_ref[0, :, hd * LANES:(hd + 1) * LANES] = jnp.where(is_qk, normed, t).astype(o_ref.dtype)


def _gdn_conv(p_main, conv_w, *, n_ctx):
    bsz, lt, _ = p_main.shape
    n = conv_w.shape[1]
    tc = 4 * LANES
    return pl.pallas_call(
        functools.partial(_gdn_conv_kernel, n_ctx=n_ctx),
        grid=(bsz, n // tc),
        in_specs=[
            pl.BlockSpec((1, lt, tc), lambda i, c: (i, 0, c)),
            pl.BlockSpec((CONV_K, tc), lambda i, c: (0, c)),
        ],
        out_specs=pl.BlockSpec((1, lt, tc), lambda i, c: (i, 0, c)),
        out_shape=jax.ShapeDtypeStruct((bsz, lt, n), BF16),
        compiler_params=_params(("parallel", "parallel")),
        name="gdn_conv",
    )(p_main, conv_w)


def _gdn_delta_kernel(qf_ref, kf_ref, vf_ref, qb_ref, kb_ref, vb_ref, colf_ref, colb_ref, rowf_ref, rowb_ref,
                      of_ref, ob_ref, s_ref):
    step = pl.program_id(1)

    @pl.when(step == 0)
    def _():
        s_ref[...] = jnp.zeros_like(s_ref)

    ii = lax.broadcasted_iota(jnp.int32, (CHUNK, CHUNK), 0)
    jj = lax.broadcasted_iota(jnp.int32, (CHUNK, CHUNK), 1)
    eye = (ii == jj).astype(F32)
    dirs = (
        (0, qf_ref, kf_ref, vf_ref, colf_ref, rowf_ref, of_ref, ii >= jj, ii > jj, CHUNK - 1),
        (1, qb_ref, kb_ref, vb_ref, colb_ref, rowb_ref, ob_ref, ii <= jj, ii < jj, 0),
    )
    chains = []
    for d, q_ref, k_ref, v_ref, col_ref, row_ref, o_ref, tri, strict, last in dirs:
        col = col_ref[0, 0]
        rows = row_ref[0, 0]
        for hd in range(GDN_HEADS):
            gi = d * GDN_HEADS + hd
            chains.append(dict(
                gi=gi, cols=slice(hd * LANES, (hd + 1) * LANES), q_ref=q_ref, k_ref=k_ref, v_ref=v_ref, o_ref=o_ref,
                tri=tri, strict=strict, last=last,
                gc=col[:, gi:gi + 1], beta=col[:, 2 * GDN_HEADS + gi:2 * GDN_HEADS + gi + 1], gr=rows[gi:gi + 1, :]))

    for c in chains:
        k = c['k_ref'][0, :, c['cols']]
        c['k16'] = k
        c['kbeta'] = k.astype(F32) * c['beta']
        c['decay'] = jnp.where(c['tri'], jnp.exp(jnp.where(c['tri'], c['gc'] - c['gr'], 0.0)), 0.0)
    for c in chains:
        c['p'] = jnp.where(c['strict'], _dot_nt(c['kbeta'].astype(BF16), c['k16']) * c['decay'], 0.0)
        c['x'] = eye - c['p']
    for _ in range((CHUNK - 1).bit_length() - 1):
        for c in chains:
            p16 = c['p'].astype(BF16)
            c['p'] = _dot(p16, p16)
        for c in chains:
            c['x'] = c['x'] + _dot(c['x'].astype(BF16), c['p'].astype(BF16))
    for c in chains:
        t_inv = c['x'].astype(BF16)
        v = c['v_ref'][0, :, c['cols']].astype(F32)
        c['u'] = _dot(t_inv, (v * c['beta']).astype(BF16))
        c['w'] = _dot(t_inv, (c['kbeta'] * jnp.exp(c['gc'])).astype(BF16))
    for c in chains:
        q = c['q_ref'][0, :, c['cols']]
        c['intra'] = (_dot_nt(q, c['k16']) * c['decay']).astype(BF16)
        c['qg'] = (q.astype(F32) * jnp.exp(c['gc'])).astype(BF16)
    for c in chains:
        c['s'] = s_ref[c['gi']]
        c['s16'] = c['s'].astype(BF16)
        c['v_new'] = (c['u'] - _dot(c['w'].astype(BF16), c['s16'])).astype(BF16)
    for c in chains:
        c['o_ref'][0, :, c['cols']] = _dot(c['qg'], c['s16']) + _dot(c['intra'], c['v_new'])
    for c in chains:
        g_last = c['gc'][c['last']:c['last'] + 1, :]
        k_dec = (c['k16'].astype(F32) * jnp.exp(g_last - c['gc'])).astype(BF16)
        s_ref[c['gi']] = c['s'] * jnp.exp(g_last) + _dot_tn(k_dec, c['v_new'])


def _gdn_delta(qkv, col_pack, row_pack, *, n_ctx):
    bsz, lt, _ = qkv.shape
    nc = lt // CHUNK
    ncc = n_ctx // CHUNK
    kw = GDN_HEADS * GDN_DK

    def bwd_chunk(c):
        return jnp.where(c < ncc, ncc - 1 - c, nc - 1 - (c - ncc))

    fwd = lambda blk: pl.BlockSpec((1, CHUNK, kw), lambda i, c: (i, c, blk))
    bwd = lambda blk: pl.BlockSpec((1, CHUNK, kw), lambda i, c: (i, bwd_chunk(c), blk))
    ncol, nrow = col_pack.shape[3], row_pack.shape[2]
    return pl.pallas_call(
        _gdn_delta_kernel,
        grid=(bsz, nc),
        in_specs=[
            fwd(0), fwd(1), fwd(2), bwd(0), bwd(1), bwd(2),
            pl.BlockSpec((1, 1, CHUNK, ncol), lambda i, c: (i, c, 0, 0)),
            pl.BlockSpec((1, 1, CHUNK, ncol), lambda i, c: (i, bwd_chunk(c), 0, 0)),
            pl.BlockSpec((1, 1, nrow, CHUNK), lambda i, c: (i, c, 0, 0)),
            pl.BlockSpec((1, 1, nrow, CHUNK), lambda i, c: (i, bwd_chunk(c), 0, 0)),
        ],
        out_specs=[
            pl.BlockSpec((1, CHUNK, kw), lambda i, c: (i, c, 0)),
            pl.BlockSpec((1, CHUNK, kw), lambda i, c: (i, bwd_chunk(c), 0)),
        ],
        out_shape=[jax.ShapeDtypeStruct((bsz, lt, kw), F32), jax.ShapeDtypeStruct((bsz, lt, kw), F32)],
        scratch_shapes=[pltpu.VMEM((2 * GDN_HEADS, GDN_DK, GDN_DK), F32)],
        compiler_params=_params(("parallel", "arbitrary")),
        name="gdn_delta",
    )(qkv, qkv, qkv, qkv, qkv, qkv, col_pack, col_pack, row_pack, row_pack)


def _gdn_gates(p_small, a_log, dt_bias):
    bsz, lt, _ = p_small.shape
    nh = GDN_HEADS
    beta = jax.nn.sigmoid(p_small[..., :2 * nh])
    a = p_small[..., 2 * nh:4 * nh]
    g = -jnp.exp(a_log.reshape(2 * nh)) * jax.nn.softplus(a + dt_bias.reshape(2 * nh))
    gch = g.reshape(bsz, lt // CHUNK, CHUNK, 2 * nh)
    gf = jnp.cumsum(gch[..., :nh], axis=2)
    gb = jnp.flip(jnp.cumsum(jnp.flip(gch[..., nh:], axis=2), axis=2), axis=2)
    gc = jnp.concatenate([gf, gb], axis=-1)
    col_pack = jnp.concatenate([gc, beta.reshape(bsz, lt // CHUNK, CHUNK, 2 * nh)], axis=-1)
    row_pack = jnp.swapaxes(gc, 2, 3)
    return col_pack, row_pack


def _rope_tables(n_ctx, seq):
    half = AXIS_DIM // 2
    lane = jnp.arange(LANES)
    inv = ROPE_THETA ** (-jnp.arange(0, AXIS_DIM, 2, dtype=F32) / AXIS_DIM)
    freq = inv[lane % half]
    use_col = (lane % HEAD_DIM) // AXIS_DIM == 1
    sign = jnp.where((lane // half) % 2 == 0, -1.0, 1.0)
    tok = jnp.arange(seq)
    pos = jnp.where(use_col[None, :], (tok % GRID_W)[:, None], (tok // GRID_W)[:, None]).astype(F32)
    ang = pos * freq[None, :]
    cos = jnp.concatenate([jnp.ones((n_ctx, LANES), F32), jnp.cos(ang)], axis=0)
    sin = jnp.concatenate([jnp.zeros((n_ctx, LANES), F32), jnp.sin(ang) * sign[None, :]], axis=0)
    return cos, sin


def _pad_cols(a, n):
    return jnp.pad(a, [(0, 0)] * (a.ndim - 1) + [(0, n - a.shape[-1])])


def kernel(x, c, ctx, c_ctx, w_mod, b_mod, g_mix, g_ffn, win_w_qkv, win_b_qkv, win_sink, win_w_o, win_b_o, glb_w_qkv, glb_g_q, glb_g_k, glb_w_o, gdn_w_in, gdn_conv_w, gdn_a_log, gdn_dt_bias, gdn_g_out, gdn_w_o, moe_w_router, moe_b_router, moe_w_up, moe_b_up, moe_w_down, moe_b_down, g_final):
    bsz, seq, d = x.shape
    n_ctx = ctx.shape[1]
    depth = w_mod.shape[0]
    assert n_ctx % ROW_TILE == 0 and seq % ROW_TILE == 0 and n_ctx >= WINDOW and seq >= ATTN_TQ + 2 * WINDOW
    nq, nk = N_HEADS * HEAD_DIM, N_KV_HEADS * HEAD_DIM

    xs = jnp.concatenate([ctx, x], axis=1)
    cond_rows = -(-(bsz + 1) // 8) * 8
    cond = jnp.zeros((cond_rows, d), F32).at[:bsz].set(c).at[bsz].set(c_ctx)
    mods = _adaln(cond, w_mod, b_mod).reshape(depth, cond_rows, 6, d)
    cos, sin = _rope_tables(n_ctx, seq)
    zero_bias = jnp.zeros((1, d), F32)
    moe_buf = jnp.zeros((_moe_rows(bsz * (n_ctx + seq)), d), F32)

    for i in range(depth):
        kind, j = i % 3, i // 3
        mod = jnp.stack([jnp.broadcast_to(mods[i, bsz], (bsz, 6, d)), mods[i, :bsz]], axis=1)
        g1 = g_mix[i].reshape(1, d)
        w_r = _pad_cols(moe_w_router[i], LANES)
        w_r_hi = w_r.astype(BF16)
        w_r = jnp.stack([w_r_hi, (w_r - w_r_hi.astype(F32)).astype(BF16)])
        b_r = jnp.concatenate([moe_b_router[i], jnp.full((LANES - N_EXPERTS,), NEG_INF, F32)]).reshape(1, LANES)
        route_args = (g_ffn[i].reshape(1, d), w_r, b_r)
        if kind == 0:
            qkv = _qkv_project(xs, mod, g1, win_w_qkv[j].astype(BF16), win_b_qkv[j].reshape(1, -1), cos, sin,
                               jnp.ones((1, nq + nk), F32), n_ctx=n_ctx, qk_norm=False)
            o = _attention(qkv, win_sink[j], windowed=True, n_ctx=n_ctx)
            outs = _mixer_out((o,), xs, mod, win_w_o[j].astype(BF16), win_b_o[j].reshape(1, d), *route_args,
                              n_ctx=n_ctx)
        elif kind == 1:
            hg = jnp.concatenate([jnp.tile(glb_g_q[j], N_HEADS), jnp.tile(glb_g_k[j], N_KV_HEADS)]).reshape(1, -1)
            qkv = _qkv_project(xs, mod, g1, glb_w_qkv[j].astype(BF16), jnp.zeros((1, nq + 2 * nk), F32), cos, sin,
                               hg, n_ctx=n_ctx, qk_norm=True)
            o = _attention(qkv, jnp.zeros((N_HEADS,), F32), windowed=False, n_ctx=n_ctx)
            outs = _mixer_out((o,), xs, mod, glb_w_o[j].astype(BF16), zero_bias, *route_args, n_ctx=n_ctx)
        else:
            n_main = 2 * GDN_HEADS * GDN_DK + 2 * GDN_HEADS * LANES
            w_in = _pad_cols(gdn_w_in[j], n_main + LANES).astype(BF16)
            p_main, p_small = _gdn_in_project(xs, mod, g1, w_in, n_ctx=n_ctx, n_main=n_main)
            qkv = _gdn_conv(p_main, gdn_conv_w[j], n_ctx=n_ctx)
            col_pack, row_pack = _gdn_gates(p_small, gdn_a_log[j], gdn_dt_bias[j])
            o_f, o_b = _gdn_delta(qkv, col_pack, row_pack, n_ctx=n_ctx)
            outs = _mixer_out((o_f, o_b, p_main), xs, mod, gdn_w_o[j].astype(BF16), zero_bias, *route_args,
                              n_ctx=n_ctx, gdn_gain=gdn_g_out[j].reshape(1, LANES))
        x_mid, h2, idx, gates = outs
        xs, moe_buf = _moe_layer(x_mid, h2, idx, gates, mod, moe_w_up, moe_b_up[i], moe_w_down, moe_b_down[i],
                                 moe_buf, n_ctx=n_ctx, layer=i,
                                 final_gain=g_final.reshape(1, d) if i == depth - 1 else None)
    return xs
```

```python
import functools

import jax
import jax.numpy as jnp
from jax import lax
from jax.experimental import pallas as pl
from jax.experimental.pallas import tpu as pltpu

F32 = jnp.float32
BF16 = jnp.bfloat16
HIGHEST = lax.Precision.HIGHEST

GRID_W = 64
HEAD_DIM = 64
N_HEADS = 16
N_KV_HEADS = 4
WINDOW = 128
ROPE_THETA = 10000.0
AXIS_DIM = HEAD_DIM // 2
GDN_HEADS = 8
GDN_DK = 128
CONV_K = 5
CHUNK = 64
N_EXPERTS = 32
TOP_K = 4
SWIGLU_LIMIT = 7.0
SWIGLU_ALPHA = 1.702
NORM_EPS = 1e-6
NEG_INF = -1e30

LANES = 128
TOKEN_ROWS = 8
ROW_TILE = 256
ATTN_TQ = 128
GLOBAL_TQ = 256
MIX_SUB = 128
MOE_TILE = 256
VMEM_LIMIT = 48 * 1024 * 1024
MOE_VMEM_LIMIT = 56 * 1024 * 1024


def _params(sem):
    return pltpu.CompilerParams(dimension_semantics=sem, vmem_limit_bytes=VMEM_LIMIT)


def _rms(x, eps=NORM_EPS):
    return x * lax.rsqrt(jnp.mean(x * x, axis=-1, keepdims=True) + eps)


def _dot(a, b):
    return jnp.dot(a, b, preferred_element_type=F32)


def _dot_nt(a, b):
    return lax.dot_general(a, b, (((1,), (1,)), ((), ())), preferred_element_type=F32)


def _dot_tn(a, b):
    return lax.dot_general(a, b, (((0,), (0,)), ((), ())), preferred_element_type=F32)


def _adaln_kernel(cond_ref, w_ref, b_ref, o_ref):
    c = cond_ref[...]
    a = (c * jax.nn.sigmoid(c)).astype(BF16)
    o_ref[0] = _dot(a, w_ref[0].astype(BF16)) + b_ref[0]


def _adaln(cond, w_mod, b_mod):
    depth, d, n = w_mod.shape
    r = cond.shape[0]
    tn = n // 4
    return pl.pallas_call(
        _adaln_kernel,
        grid=(depth, n // tn),
        in_specs=[
            pl.BlockSpec((r, d), lambda i, j: (0, 0)),
            pl.BlockSpec((1, d, tn), lambda i, j: (i, 0, j)),
            pl.BlockSpec((1, 1, tn), lambda i, j: (i, 0, j)),
        ],
        out_specs=pl.BlockSpec((1, r, tn), lambda i, j: (i, 0, j)),
        out_shape=jax.ShapeDtypeStruct((depth, r, n), F32),
        compiler_params=_params(("parallel", "parallel")),
        name="adaln",
    )(cond, w_mod, b_mod.reshape(depth, 1, n))


def _norm_mod(x, m, g, which):
    return _rms(x) * g * (1.0 + m[which + 1:which + 2]) + m[which:which + 1]


def _swap16(t):
    lane = lax.broadcasted_iota(jnp.int32, t.shape, 1)
    first = (lane // (AXIS_DIM // 2)) % 2 == 0
    return jnp.where(first, pltpu.roll(t, LANES - AXIS_DIM // 2, 1), pltpu.roll(t, AXIS_DIM // 2, 1))


def _head_sum_matrix():
    r = lax.broadcasted_iota(jnp.int32, (LANES, LANES), 0) // HEAD_DIM
    c = lax.broadcasted_iota(jnp.int32, (LANES, LANES), 1) // HEAD_DIM
    return (r == c).astype(F32)


def _qkv_kernel(x_ref, mod_ref, g_ref, w_ref, b_ref, cos_ref, sin_ref, hg_ref, o_ref, *, qk_norm):
    nq = N_HEADS * HEAD_DIM
    nk = N_KV_HEADS * HEAD_DIM
    h = _norm_mod(x_ref[0], mod_ref[0, 0], g_ref[...], 0)
    p = _dot(h.astype(BF16), w_ref[...]) + b_ref[...]
    cos = cos_ref[...]
    sin = sin_ref[...]
    lane = lax.broadcasted_iota(jnp.int32, cos.shape, 1)
    low = lane < HEAD_DIM
    if qk_norm:
        seg = _head_sum_matrix()

    def expand(t, base, ones_lane):
        r = pltpu.roll(t, HEAD_DIM, 1)
        fill_hi = jnp.where(lane == HEAD_DIM, 1.0, 0.0) if ones_lane else jnp.zeros_like(t)
        fill_lo = jnp.where(lane == 0, 1.0, 0.0) if ones_lane else jnp.zeros_like(t)
        parts = (jnp.where(low, t, fill_hi), jnp.where(low, fill_lo, r),
                 jnp.where(low, r, fill_hi), jnp.where(low, fill_lo, t))
        for i, part in enumerate(parts):
            o_ref[0, :, base + i * LANES: base + (i + 1) * LANES] = part.astype(o_ref.dtype)

    for j in range((nq + nk) // LANES):
        t = p[:, j * LANES:(j + 1) * LANES]
        if qk_norm:
            ss = jnp.dot(t * t, seg, precision=HIGHEST, preferred_element_type=F32)
            t = t * lax.rsqrt(ss * (1.0 / HEAD_DIM) + NORM_EPS) * hg_ref[:, j * LANES:(j + 1) * LANES]
        t = t * cos + _swap16(t) * sin
        if j < nq // LANES:
            o_ref[0, :, j * LANES:(j + 1) * LANES] = (t * (HEAD_DIM ** -0.5)).astype(o_ref.dtype)
        else:
            expand(t, nq + (j - nq // LANES) * 4 * LANES, False)
    for j in range(nk // LANES):
        t = p[:, nq + nk + j * LANES: nq + nk + (j + 1) * LANES]
        expand(t, nq + 4 * nk + j * 4 * LANES, True)


def _qkv_project(x, mod, g, w, b, cos, sin, hg, *, n_ctx, qk_norm):
    bsz, lt, d = x.shape
    n = w.shape[1]
    tm = ROW_TILE
    nct = n_ctx // tm
    n_out = N_HEADS * HEAD_DIM + 8 * N_KV_HEADS * HEAD_DIM
    return pl.pallas_call(
        functools.partial(_qkv_kernel, qk_norm=qk_norm),
        grid=(bsz, lt // tm),
        in_specs=[
            pl.BlockSpec((1, tm, d), lambda i, t: (i, t, 0)),
            pl.BlockSpec((1, 1, 6, d), lambda i, t: (i, (t >= nct).astype(jnp.int32), 0, 0)),
            pl.BlockSpec((1, d), lambda i, t: (0, 0)),
            pl.BlockSpec((d, n), lambda i, t: (0, 0)),
            pl.BlockSpec((1, n), lambda i, t: (0, 0)),
            pl.BlockSpec((tm, LANES), lambda i, t: (t, 0)),
            pl.BlockSpec((tm, LANES), lambda i, t: (t, 0)),
            pl.BlockSpec((1, hg.shape[1]), lambda i, t: (0, 0)),
        ],
        out_specs=pl.BlockSpec((1, tm, n_out), lambda i, t: (i, t, 0)),
        out_shape=jax.ShapeDtypeStruct((bsz, lt, n_out), BF16),
        compiler_params=_params(("parallel", "parallel")),
        name="qkv_project",
    )(x, mod, g, w, b, cos, sin, hg)


def _gdn_in_kernel(x_ref, mod_ref, g_ref, w_ref, o_ref, o2_ref):
    h = _norm_mod(x_ref[0], mod_ref[0, 0], g_ref[...], 0)
    p = _dot(h.astype(BF16), w_ref[...])
    n_main = o_ref.shape[2]
    o_ref[0] = p[:, :n_main].astype(o_ref.dtype)
    o2_ref[0] = p[:, n_main:]


def _gdn_in_project(x, mod, g, w, *, n_ctx, n_main):
    bsz, lt, d = x.shape
    n = w.shape[1]
    tm = ROW_TILE
    nct = n_ctx // tm
    return pl.pallas_call(
        _gdn_in_kernel,
        grid=(bsz, lt // tm),
        in_specs=[
            pl.BlockSpec((1, tm, d), lambda i, t: (i, t, 0)),
            pl.BlockSpec((1, 1, 6, d), lambda i, t: (i, (t >= nct).astype(jnp.int32), 0, 0)),
            pl.BlockSpec((1, d), lambda i, t: (0, 0)),
            pl.BlockSpec((d, n), lambda i, t: (0, 0)),
        ],
        out_specs=[
            pl.BlockSpec((1, tm, n_main), lambda i, t: (i, t, 0)),
            pl.BlockSpec((1, tm, n - n_main), lambda i, t: (i, t, 0)),
        ],
        out_shape=[
            jax.ShapeDtypeStruct((bsz, lt, n_main), BF16),
            jax.ShapeDtypeStruct((bsz, lt, n - n_main), F32),
        ],
        compiler_params=_params(("parallel", "parallel")),
        name="gdn_in_project",
    )(x, mod, g, w)


def _attn_kernel(sink_ref, q_ref, k_ref, v_ref, o_ref, *, windowed, n_ctx, seq):
    tq = q_ref.shape[1]
    t = pl.program_id(1)
    nct = n_ctx // tq
    gw = 2 * LANES
    top = lax.broadcasted_iota(jnp.int32, (2 * tq, 1), 0) < tq
    low = lax.broadcasted_iota(jnp.int32, (2 * tq, LANES), 1) < HEAD_DIM

    def run(segments, batch):
        for g0 in range(0, N_KV_HEADS, batch):
            heads = range(g0, g0 + batch)
            jobs = [(g, half) for g in heads for half in range(2)]
            qs = {g: jnp.concatenate([q_ref[0, :, g * gw:g * gw + LANES],
                                      q_ref[0, :, g * gw + LANES:(g + 1) * gw]], axis=0) for g in heads}
            scores, ms, outs = {}, {}, {}
            for g, half in jobs:
                cols = slice(g * gw + half * LANES, g * gw + (half + 1) * LANES)
                scores[g, half] = []
                for rows, mask in segments:
                    s = _dot_nt(qs[g], k_ref[0, rows, cols])
                    if mask is not None:
                        s = jnp.where(mask, s, NEG_INF)
                    scores[g, half].append(s)
            for g, half in jobs:
                m = scores[g, half][0].max(axis=-1, keepdims=True)
                for s in scores[g, half][1:]:
                    m = jnp.maximum(m, s.max(axis=-1, keepdims=True))
                if windowed:
                    m = jnp.maximum(m, jnp.where(top, sink_ref[g * 4 + half], sink_ref[g * 4 + 2 + half]))
                ms[g, half] = m
            for g, half in jobs:
                cols = slice(g * gw + half * LANES, g * gw + (half + 1) * LANES)
                o = None
                for s, (rows, _) in zip(scores[g, half], segments):
                    p = jnp.exp((s - ms[g, half]).astype(BF16))
                    pv = _dot(p, v_ref[0, rows, cols])
                    o = pv if o is None else o + pv
                outs[g, half] = o
            for g in heads:
                den_lo = outs[g, 0][:, HEAD_DIM:HEAD_DIM + 1]
                den_hi = outs[g, 1][:, 0:1]
                if windowed:
                    den_lo = den_lo + jnp.exp(jnp.where(top, sink_ref[g * 4], sink_ref[g * 4 + 2]) - ms[g, 0])
                    den_hi = den_hi + jnp.exp(jnp.where(top, sink_ref[g * 4 + 1], sink_ref[g * 4 + 3]) - ms[g, 1])
                acc = jnp.where(low, outs[g, 0] * (1.0 / den_lo), outs[g, 1] * (1.0 / den_hi))
                o_ref[0, :, g * gw:g * gw + LANES] = acc[:tq].astype(o_ref.dtype)
                o_ref[0, :, g * gw + LANES:(g + 1) * gw] = acc[tq:].astype(o_ref.dtype)

    ctx_rows = slice(0, n_ctx)

    @pl.when(t < nct)
    def _():
        run([(ctx_rows, None)], N_KV_HEADS)

    @pl.when(t >= nct)
    def _():
        if windowed:
            span = tq + 2 * WINDOW
            start = (t - nct) * tq
            r0 = pl.multiple_of(jnp.minimum(n_ctx + start - WINDOW, n_ctx + seq - span), LANES)
            qpos = start + lax.broadcasted_iota(jnp.int32, (tq, span), 0)
            kpos = r0 - n_ctx + lax.broadcasted_iota(jnp.int32, (tq, span), 1)
            band = (jnp.abs(qpos - kpos) <= WINDOW) & (kpos >= 0)
            band = jnp.concatenate([band, band], axis=0)
            run([(pl.ds(r0, span), band), (ctx_rows, None)], N_KV_HEADS)
        else:
            run([(slice(0, n_ctx + seq), None)], 1)


def _attention(qkv, sink, *, windowed, n_ctx):
    bsz, lt, _ = qkv.shape
    tq = ATTN_TQ if windowed else GLOBAL_TQ
    nq = N_HEADS * HEAD_DIM
    kernel = functools.partial(_attn_kernel, windowed=windowed, n_ctx=n_ctx, seq=lt - n_ctx)
    return pl.pallas_call(
        kernel,
        grid=(bsz, lt // tq),
        in_specs=[
            pl.BlockSpec(memory_space=pltpu.SMEM),
            pl.BlockSpec((1, tq, nq), lambda i, t: (i, t, 0)),
            pl.BlockSpec((1, lt, nq), lambda i, t: (i, 0, 1)),
            pl.BlockSpec((1, lt, nq), lambda i, t: (i, 0, 2)),
        ],
        out_specs=pl.BlockSpec((1, tq, nq), lambda i, t: (i, t, 0)),
        out_shape=jax.ShapeDtypeStruct((bsz, lt, nq), BF16),
        compiler_params=_params(("parallel", "arbitrary")),
        name="window_attention" if windowed else "global_attention",
    )(sink, qkv, qkv, qkv)


def _sub_tiles(tm):
    return [slice(r, r + MIX_SUB) for r in range(0, tm, MIX_SUB)]


def _mix_out_tail(ys, x_ref, m, gffn_ref, wr_ref, br_ref, xmid_ref, h2_ref, idx_ref, gate_ref):
    g = gffn_ref[...]
    w_hi, w_lo, b_r = wr_ref[0], wr_ref[1], br_ref[...]
    hs = []
    for rows, y in ys:
        xm = x_ref[0, rows, :] + m[2:3] * y
        xmid_ref[0, rows, :] = xm
        h = _norm_mod(xm, m, g, 3)
        for s in range(TOKEN_ROWS):
            h2_ref[0, pl.ds(rows.start * TOKEN_ROWS + s, rows.stop - rows.start, stride=TOKEN_ROWS), :] = (
                h[:, s * LANES:(s + 1) * LANES])
        hs.append(h)
    logits = []
    for h in hs:
        h_hi = h.astype(BF16)
        h_lo = (h - h_hi.astype(F32)).astype(BF16)
        logits.append(_dot(h_hi, w_hi) + (_dot(h_hi, w_lo) + _dot(h_lo, w_hi)) + b_r)
    lane = lax.broadcasted_iota(jnp.int32, logits[0].shape, 1)
    vals = [[] for _ in ys]
    idxs = [[] for _ in ys]
    for _ in range(TOP_K):
        for s in range(len(ys)):
            mx = logits[s].max(axis=-1, keepdims=True)
            ix = jnp.where(logits[s] == mx, lane, LANES).min(axis=-1, keepdims=True)
            vals[s].append(mx)
            idxs[s].append(ix)
            logits[s] = jnp.where(lane == ix, -jnp.inf, logits[s])
    for s, (rows, _) in enumerate(ys):
        es = [jnp.exp(v - vals[s][0]) for v in vals[s]]
        inv = 1.0 / (es[0] + es[1] + es[2] + es[3])
        gates = jnp.zeros(lane.shape, F32)
        idx = jnp.zeros(lane.shape, jnp.int32)
        for k in range(TOP_K):
            gates = jnp.where(lane == k, es[k] * inv, gates)
            idx = jnp.where(lane == k, idxs[s][k], idx)
        idx_ref[0, rows, :] = idx
        gate_ref[0, rows, :] = gates


def _attn_out_kernel(o_ref, x_ref, mod_ref, wo_ref, bo_ref, gffn_ref, wr_ref, br_ref,
                     xmid_ref, h2_ref, idx_ref, gate_ref):
    ys = [(rows, _dot(o_ref[0, rows, :], wo_ref[...]) + bo_ref[...]) for rows in _sub_tiles(o_ref.shape[1])]
    _mix_out_tail(ys, x_ref, mod_ref[0, 0], gffn_ref, wr_ref, br_ref, xmid_ref, h2_ref, idx_ref, gate_ref)


def _gdn_out_kernel(of_ref, ob_ref, z_ref, gout_ref, x_ref, mod_ref, wo_ref, bo_ref, gffn_ref, wr_ref, br_ref,
                    xmid_ref, h2_ref, idx_ref, gate_ref):
    gout = gout_ref[...]
    ys = []
    for rows in _sub_tiles(of_ref.shape[1]):
        parts = []
        for hd in range(GDN_HEADS):
            cols = slice(hd * LANES, (hd + 1) * LANES)
            z = z_ref[0, rows, cols].astype(F32)
            o = _rms(of_ref[0, rows, cols] + ob_ref[0, rows, cols]) * gout
            parts.append((o * (z * jax.nn.sigmoid(z))).astype(BF16))
        ys.append((rows, _dot(jnp.concatenate(parts, axis=1), wo_ref[...]) + bo_ref[...]))
    _mix_out_tail(ys, x_ref, mod_ref[0, 0], gffn_ref, wr_ref, br_ref, xmid_ref, h2_ref, idx_ref, gate_ref)


def _mixer_out(o_inputs, x, mod, w_o, b_o, g_ffn, w_r, b_r, *, n_ctx, gdn_gain=None):
    bsz, lt, d = x.shape
    tm = ROW_TILE
    nct = n_ctx // tm
    row = lambda i, t: (i, t, 0)
    const = lambda i, t: (0, 0)
    if gdn_gain is None:
        kernel = _attn_out_kernel
        head_specs = [pl.BlockSpec((1, tm, d), row)]
        head_args = list(o_inputs)
    else:
        kernel = _gdn_out_kernel
        o_f, o_b, p_main = o_inputs
        zblk = p_main.shape[2] // d - 1
        head_specs = [pl.BlockSpec((1, tm, d), row), pl.BlockSpec((1, tm, d), row),
                      pl.BlockSpec((1, tm, d), lambda i, t: (i, t, zblk)),
                      pl.BlockSpec((1, LANES), const)]
        head_args = [o_f, o_b, p_main, gdn_gain]
    return pl.pallas_call(
        kernel,
        grid=(bsz, lt // tm),
        in_specs=head_specs + [
            pl.BlockSpec((1, tm, d), row),
            pl.BlockSpec((1, 1, 6, d), lambda i, t: (i, (t >= nct).astype(jnp.int32), 0, 0)),
            pl.BlockSpec((d, d), const),
            pl.BlockSpec((1, d), const),
            pl.BlockSpec((1, d), const),
            pl.BlockSpec((2, d, LANES), lambda i, t: (0, 0, 0)),
            pl.BlockSpec((1, LANES), const),
        ],
        out_specs=[
            pl.BlockSpec((1, tm, d), row),
            pl.BlockSpec((1, tm * TOKEN_ROWS, LANES), row),
            pl.BlockSpec((1, tm, LANES), row),
            pl.BlockSpec((1, tm, LANES), row),
        ],
        out_shape=[
            jax.ShapeDtypeStruct((bsz, lt, d), F32),
            jax.ShapeDtypeStruct((bsz, lt * TOKEN_ROWS, LANES), F32),
            jax.ShapeDtypeStruct((bsz, lt, LANES), jnp.int32),
            jax.ShapeDtypeStruct((bsz, lt, LANES), F32),
        ],
        compiler_params=_params(("parallel", "parallel")),
        name="mixer_out_router",
    )(*head_args, x, mod, w_o, b_o, g_ffn, w_r, b_r)


def _route_rank_kernel(idx_ref, rank_ref, cnt_ref, run_ref):
    i = pl.program_id(0)

    @pl.when(i == 0)
    def _():
        run_ref[...] = jnp.zeros_like(run_ref)

    idx = idx_ref[...]
    tm = idx.shape[0]
    lane = lax.broadcasted_iota(jnp.int32, idx.shape, 1)
    earlier = (lax.broadcasted_iota(jnp.int32, (tm, tm), 1) < lax.broadcasted_iota(jnp.int32, (tm, tm), 0)).astype(BF16)
    base = run_ref[...]
    rank = jnp.zeros(idx.shape, jnp.int32)
    for k in range(TOP_K):
        hit = lane == idx[:, k:k + 1]
        before = _dot(earlier, hit.astype(BF16)) + base
        r = jnp.sum(jnp.where(hit, before, 0.0), axis=-1, keepdims=True)
        rank = jnp.where(lane == k, r.astype(jnp.int32), rank)
        base = base + jnp.sum(hit.astype(F32), axis=0, keepdims=True)
    rank_ref[...] = rank
    run_ref[...] = base
    cnt_ref[...] = jnp.broadcast_to(base, cnt_ref.shape)


def _route_rank(idx):
    n = idx.shape[0]
    tm = 2 * ROW_TILE if n % (2 * ROW_TILE) == 0 else ROW_TILE
    return pl.pallas_call(
        _route_rank_kernel,
        grid=(n // tm,),
        in_specs=[pl.BlockSpec((tm, LANES), lambda i: (i, 0))],
        out_specs=[pl.BlockSpec((tm, LANES), lambda i: (i, 0)), pl.BlockSpec((8, LANES), lambda i: (0, 0))],
        out_shape=[jax.ShapeDtypeStruct((n, LANES), jnp.int32), jax.ShapeDtypeStruct((8, LANES), F32)],
        scratch_shapes=[pltpu.VMEM((1, LANES), F32)],
        compiler_params=_params(("arbitrary",)),
        name="route_rank",
    )(idx)


def _dispatch_kernel(dest_ref, h_ref, xs_in_ref, xs_ref, sem):
    del xs_in_ref
    tm = h_ref.shape[0] // TOKEN_ROWS

    def row_copy(r, dst):
        src = h_ref.at[pl.ds(pl.multiple_of(r * TOKEN_ROWS, TOKEN_ROWS), TOKEN_ROWS)]
        return pltpu.make_async_copy(src, xs_ref.at[pl.ds(pl.multiple_of(dst * TOKEN_ROWS, TOKEN_ROWS), TOKEN_ROWS)], sem)

    def issue(r, carry):
        for k in range(TOP_K):
            row_copy(r, dest_ref[0, 0, r * TOP_K + k]).start(priority=k % 2)
        return carry

    lax.fori_loop(0, tm, issue, 0, unroll=8)
    def drain(r, carry):
        for k in range(TOP_K):
            row_copy(r, 0).wait()
        return carry

    lax.fori_loop(0, tm, drain, 0, unroll=8)


def _dispatch(h2, dest, xs_buf):
    n = dest.shape[0]
    tm = ROW_TILE
    dest3 = dest.reshape(n // tm, 1, tm * TOP_K)
    return pl.pallas_call(
        _dispatch_kernel,
        grid=(n // tm,),
        in_specs=[
            pl.BlockSpec((1, 1, tm * TOP_K), lambda i: (i, 0, 0), memory_space=pltpu.SMEM),
            pl.BlockSpec((tm * TOKEN_ROWS, LANES), lambda i: (i, 0)),
            pl.BlockSpec(memory_space=pl.ANY),
        ],
        out_specs=pl.BlockSpec(memory_space=pl.ANY),
        out_shape=jax.ShapeDtypeStruct(xs_buf.shape, h2.dtype),
        scratch_shapes=[pltpu.SemaphoreType.DMA],
        input_output_aliases={2: 0},
        compiler_params=_params(("arbitrary",)),
        name="moe_dispatch",
    )(dest3, h2, xs_buf)


def _moe_kernel(te_ref, nu_ref, first_ref, nxt_ref, slot_ref, x_ref, bu_ref, bd_ref, wu_hbm, wd_hbm, o_ref,
                wu_f32, wd_f32, wu16, wd16, sem, *, layer):
    i = pl.program_id(0)
    de = wd16.shape[0]

    def weight_copies(expert, slot):
        return (pltpu.make_async_copy(wu_hbm.at[layer, expert], wu_f32.at[slot], sem.at[0, slot]),
                pltpu.make_async_copy(wd_hbm.at[layer, expert], wd_f32.at[slot], sem.at[1, slot]))

    active = i < nu_ref[0]

    @pl.when(active & (i == 0))
    def _():
        for cp in weight_copies(te_ref[0], 0):
            cp.start()

    @pl.when(active & (first_ref[i] == 1))
    def _():
        slot = slot_ref[i]
        for cp in weight_copies(te_ref[i], slot):
            cp.wait()

        @pl.when(nxt_ref[i] >= 0)
        def _():
            for cp in weight_copies(nxt_ref[i], 1 - slot):
                cp.start()

        wu16[...] = wu_f32[slot].astype(BF16)
        wd16[...] = wd_f32[slot].astype(BF16)

    @pl.when(active)
    def _():
        tm = o_ref.shape[0]
        x = jnp.concatenate([x_ref[pl.ds(s, tm, stride=TOKEN_ROWS), :].astype(BF16) for s in range(TOKEN_ROWS)], axis=1)
        u = _dot(x, wu16[...]) + bu_ref[0]
        glu = jnp.minimum(u[:, :de], SWIGLU_LIMIT)
        lin = jnp.clip(u[:, de:], -SWIGLU_LIMIT, SWIGLU_LIMIT)
        act = glu * jax.nn.sigmoid(SWIGLU_ALPHA * glu) * (lin + 1.0)
        o_ref[...] = (_dot(act.astype(BF16), wd16[...]) + bd_ref[0]).astype(o_ref.dtype)

    @pl.when(jnp.logical_not(active))
    def _():
        o_ref[...] = jnp.zeros_like(o_ref)


def _moe_experts(xs, tile_expert, n_used, first, nxt, slot, w_up, b_up, w_down, b_down, *, layer):
    rows = xs.shape[0] // TOKEN_ROWS
    tm = MOE_TILE
    n_tiles = rows // tm
    _, ne, d, n_up = w_up.shape
    de = w_down.shape[2]
    grid_spec = pltpu.PrefetchScalarGridSpec(
        num_scalar_prefetch=5,
        grid=(n_tiles,),
        in_specs=[
            pl.BlockSpec((tm * TOKEN_ROWS, LANES), lambda i, te, nu, *_: (jnp.minimum(i, nu[0] - 1), 0)),
            pl.BlockSpec((1, 1, n_up), lambda i, te, *_: (te[i], 0, 0)),
            pl.BlockSpec((1, 1, d), lambda i, te, *_: (te[i], 0, 0)),
            pl.BlockSpec(memory_space=pl.ANY),
            pl.BlockSpec(memory_space=pl.ANY),
        ],
        out_specs=pl.BlockSpec((tm, d), lambda i, *_: (i, 0)),
        scratch_shapes=[
            pltpu.VMEM((2, d, n_up), F32),
            pltpu.VMEM((2, de, d), F32),
            pltpu.VMEM((d, n_up), BF16),
            pltpu.VMEM((de, d), BF16),
            pltpu.SemaphoreType.DMA((2, 2)),
        ],
    )
    return pl.pallas_call(
        functools.partial(_moe_kernel, layer=layer),
        grid_spec=grid_spec,
        out_shape=jax.ShapeDtypeStruct((rows, d), BF16),
        compiler_params=pltpu.CompilerParams(dimension_semantics=("arbitrary",), vmem_limit_bytes=MOE_VMEM_LIMIT),
        name="moe_experts",
    )(tile_expert, n_used, first, nxt, slot, xs, b_up.reshape(ne, 1, n_up), b_down.reshape(ne, 1, d), w_up, w_down)


def _combine_kernel(x_ref, y0_ref, y1_ref, y2_ref, y3_ref, gate_ref, mod_ref, gfin_ref, o_ref, *, final):
    gates = gate_ref[0]
    acc = gates[:, 0:1] * y0_ref[...].astype(F32)
    for k, y_ref in ((1, y1_ref), (2, y2_ref), (3, y3_ref)):
        acc = acc + gates[:, k:k + 1] * y_ref[...].astype(F32)
    x_new = x_ref[0] + mod_ref[0, 0][5:6] * acc
    o_ref[0] = _rms(x_new) * gfin_ref[...] if final else x_new


def _moe_combine(x, yg, gates, mod, *, n_ctx, final_gain=None):
    bsz, lt, d = x.shape
    tm = ROW_TILE
    nct = n_ctx // tm
    nt = lt // tm
    final = final_gain is not None
    skip = nct if final else 0
    row = lambda i, t: (i, t + skip, 0)
    slot = lambda k: pl.BlockSpec((tm, d), lambda i, t: (k * bsz * nt + i * nt + t + skip, 0))
    return pl.pallas_call(
        functools.partial(_combine_kernel, final=final),
        grid=(bsz, nt - skip),
        in_specs=[
            pl.BlockSpec((1, tm, d), row),
            slot(0), slot(1), slot(2), slot(3),
            pl.BlockSpec((1, tm, LANES), row),
            pl.BlockSpec((1, 1, 6, d), lambda i, t: (i, (t + skip >= nct).astype(jnp.int32), 0, 0)),
            pl.BlockSpec((1, d), lambda i, t: (0, 0)),
        ],
        out_specs=pl.BlockSpec((1, tm, d), lambda i, t: (i, t, 0)),
        out_shape=jax.ShapeDtypeStruct((bsz, lt - skip * tm, d), F32),
        compiler_params=_params(("parallel", "parallel")),
        name="moe_combine",
    )(x, yg, yg, yg, yg, gates, mod, final_gain if final else jnp.ones((1, d), F32))


def _moe_rows(n):
    return (-(-(n * TOP_K) // MOE_TILE) + N_EXPERTS) * MOE_TILE


def _moe_layer(x_mid, h2, idx, gates, mod, w_up, b_up, w_down, b_down, xs_buf, *, n_ctx, layer, final_gain=None):
    bsz, lt, d = x_mid.shape
    n = bsz * lt
    tm = MOE_TILE
    n_tiles = xs_buf.shape[0] // (tm * TOKEN_ROWS)
    idx2 = idx.reshape(n, LANES)
    rank, cnt = _route_rank(idx2)
    counts = cnt[0, :N_EXPERTS].astype(jnp.int32)
    padded = (counts + tm - 1) // tm * tm
    pend = jnp.cumsum(padded)
    pstart = pend - padded
    experts = idx2[:, :TOP_K]
    onehot = experts[:, :, None] == jnp.arange(N_EXPERTS, dtype=jnp.int32)[None, None, :]
    dest = jnp.sum(jnp.where(onehot, pstart[None, None, :], 0), axis=-1) + rank[:, :TOP_K]
    n_used = (pend[-1] // tm).astype(jnp.int32)
    tile_start = jnp.arange(n_tiles, dtype=jnp.int32) * tm
    tile_e = jnp.sum((tile_start[:, None] >= pend[None, :]).astype(jnp.int32), axis=1)
    last_e = jnp.sum((jnp.maximum(n_used - 1, 0) * tm >= pend).astype(jnp.int32))
    tile_e = jnp.minimum(jnp.where(jnp.arange(n_tiles) < n_used, tile_e, last_e), N_EXPERTS - 1)
    prev_e = jnp.concatenate([jnp.full((1,), -1, jnp.int32), tile_e[:-1]])
    first = (tile_e != prev_e).astype(jnp.int32)
    slot = (jnp.cumsum(first) - 1) % 2
    ids = jnp.arange(N_EXPERTS, dtype=jnp.int32)
    later_used = (ids[None, :] > ids[:, None]) & (counts[None, :] > 0)
    next_used = jnp.min(jnp.where(later_used, ids[None, :], N_EXPERTS), axis=1)
    next_used = jnp.where(next_used == N_EXPERTS, -1, next_used)
    nxt = jnp.sum(jnp.where(tile_e[:, None] == ids[None, :], next_used[None, :], 0), axis=1)
    xs = _dispatch(h2.reshape(n * TOKEN_ROWS, LANES), dest, xs_buf)
    ys = _moe_experts(xs, tile_e, n_used.reshape(1), first, nxt, slot, w_up, b_up, w_down, b_down, layer=layer)
    yg = ys.at[dest.T.reshape(-1)].get(mode='promise_in_bounds')
    return _moe_combine(x_mid, yg, gates, mod, n_ctx=n_ctx, final_gain=final_gain), xs


def _gdn_conv_kernel(p_ref, w_ref, o_ref, *, n_ctx):
    c = pl.program_id(1)
    lt, tc = p_ref.shape[1], p_ref.shape[2]
    x = p_ref[0].astype(F32)
    row = lax.broadcasted_iota(jnp.int32, (lt, 1), 0)
    is_ctx = row < n_ctx
    pos = jnp.where(is_ctx, row, row - n_ctx)
    seg_len = jnp.where(is_ctx, n_ctx, lt - n_ctx)
    acc = x * w_ref[CONV_K // 2:CONV_K // 2 + 1, :]
    for j in range(CONV_K):
        dlt = j - CONV_K // 2
        if dlt == 0:
            continue
        shifted = pltpu.roll(x, (-dlt) % lt, 0)
        valid = (pos + dlt >= 0) & (pos + dlt < seg_len)
        acc = acc + jnp.where(valid, shifted, 0.0) * w_ref[j:j + 1, :]
    y = acc * jax.nn.sigmoid(acc)
    n_qk_tiles = 2 * GDN_HEADS * GDN_DK // tc
    is_qk = c < n_qk_tiles
    scale = jnp.where(c < n_qk_tiles // 2, GDN_DK ** -0.5, 1.0)
    for hd in range(tc // LANES):
        t = y[:, hd * LANES:(hd + 1) * LANES]
        normed = t * lax.rsqrt(jnp.sum(t * t, axis=-1, keepdims=True) + NORM_EPS) * scale
        o_ref[0, :, hd * LANES:(hd + 1) * LANES] = jnp.where(is_qk, normed, t).astype(o_ref.dtype)


def _gdn_conv(p_main, conv_w, *, n_ctx):
    bsz, lt, _ = p_main.shape
    n = conv_w.shape[1]
    tc = 4 * LANES
    return pl.pallas_call(
        functools.partial(_gdn_conv_kernel, n_ctx=n_ctx),
        grid=(bsz, n // tc),
        in_specs=[
            pl.BlockSpec((1, lt, tc), lambda i, c: (i, 0, c)),
            pl.BlockSpec((CONV_K, tc), lambda i, c: (0, c)),
        ],
        out_specs=pl.BlockSpec((1, lt, tc), lambda i, c: (i, 0, c)),
        out_shape=jax.ShapeDtypeStruct((bsz, lt, n), BF16),
        compiler_params=_params(("parallel", "parallel")),
        name="gdn_conv",
    )(p_main, conv_w)


def _gdn_delta_kernel(qf_ref, kf_ref, vf_ref, qb_ref, kb_ref, vb_ref, colf_ref, colb_ref, rowf_ref, rowb_ref,
                      of_ref, ob_ref, s_ref):
    step = pl.program_id(1)

    @pl.when(step == 0)
    def _():
        s_ref[...] = jnp.zeros_like(s_ref)

    ii = lax.broadcasted_iota(jnp.int32, (CHUNK, CHUNK), 0)
    jj = lax.broadcasted_iota(jnp.int32, (CHUNK, CHUNK), 1)
    eye = (ii == jj).astype(F32)
    dirs = (
        (0, qf_ref, kf_ref, vf_ref, colf_ref, rowf_ref, of_ref, ii >= jj, ii > jj, CHUNK - 1),
        (1, qb_ref, kb_ref, vb_ref, colb_ref, rowb_ref, ob_ref, ii <= jj, ii < jj, 0),
    )
    chains = []
    for d, q_ref, k_ref, v_ref, col_ref, row_ref, o_ref, tri, strict, last in dirs:
        col = col_ref[0, 0]
        rows = row_ref[0, 0]
        for hd in range(GDN_HEADS):
            gi = d * GDN_HEADS + hd
            chains.append(dict(
                gi=gi, cols=slice(hd * LANES, (hd + 1) * LANES), q_ref=q_ref, k_ref=k_ref, v_ref=v_ref, o_ref=o_ref,
                tri=tri, strict=strict, last=last,
                gc=col[:, gi:gi + 1], beta=col[:, 2 * GDN_HEADS + gi:2 * GDN_HEADS + gi + 1], gr=rows[gi:gi + 1, :]))

    for c in chains:
        k = c['k_ref'][0, :, c['cols']]
        c['k16'] = k
        c['kbeta'] = k.astype(F32) * c['beta']
        c['decay'] = jnp.where(c['tri'], jnp.exp(jnp.where(c['tri'], c['gc'] - c['gr'], 0.0)), 0.0)
    for c in chains:
        c['p'] = jnp.where(c['strict'], _dot_nt(c['kbeta'].astype(BF16), c['k16']) * c['decay'], 0.0)
        c['x'] = eye - c['p']
    for _ in range((CHUNK - 1).bit_length() - 1):
        for c in chains:
            p16 = c['p'].astype(BF16)
            c['p'] = _dot(p16, p16)
        for c in chains:
            c['x'] = c['x'] + _dot(c['x'].astype(BF16), c['p'].astype(BF16))
    for c in chains:
        t_inv = c['x'].astype(BF16)
        v = c['v_ref'][0, :, c['cols']].astype(F32)
        c['u'] = _dot(t_inv, (v * c['beta']).astype(BF16))
        c['w'] = _dot(t_inv, (c['kbeta'] * jnp.exp(c['gc'])).astype(BF16))
    for c in chains:
        q = c['q_ref'][0, :, c['cols']]
        c['intra'] = (_dot_nt(q, c['k16']) * c['decay']).astype(BF16)
        c['qg'] = (q.astype(F32) * jnp.exp(c['gc'])).astype(BF16)
    for c in chains:
        c['s'] = s_ref[c['gi']]
        c['s16'] = c['s'].astype(BF16)
        c['v_new'] = (c['u'] - _dot(c['w'].astype(BF16), c['s16'])).astype(BF16)
    for c in chains:
        c['o_ref'][0, :, c['cols']] = _dot(c['qg'], c['s16']) + _dot(c['intra'], c['v_new'])
    for c in chains:
        g_last = c['gc'][c['last']:c['last'] + 1, :]
        k_dec = (c['k16'].astype(F32) * jnp.exp(g_last - c['gc'])).astype(BF16)
        s_ref[c['gi']] = c['s'] * jnp.exp(g_last) + _dot_tn(k_dec, c['v_new'])


def _gdn_delta(qkv, col_pack, row_pack, *, n_ctx):
    bsz, lt, _ = qkv.shape
    nc = lt // CHUNK
    ncc = n_ctx // CHUNK
    kw = GDN_HEADS * GDN_DK

    def bwd_chunk(c):
        return jnp.where(c < ncc, ncc - 1 - c, nc - 1 - (c - ncc))

    fwd = lambda blk: pl.BlockSpec((1, CHUNK, kw), lambda i, c: (i, c, blk))
    bwd = lambda blk: pl.BlockSpec((1, CHUNK, kw), lambda i, c: (i, bwd_chunk(c), blk))
    ncol, nrow = col_pack.shape[3], row_pack.shape[2]
    return pl.pallas_call(
        _gdn_delta_kernel,
        grid=(bsz, nc),
        in_specs=[
            fwd(0), fwd(1), fwd(2), bwd(0), bwd(1), bwd(2),
            pl.BlockSpec((1, 1, CHUNK, ncol), lambda i, c: (i, c, 0, 0)),
            pl.BlockSpec((1, 1, CHUNK, ncol), lambda i, c: (i, bwd_chunk(c), 0, 0)),
            pl.BlockSpec((1, 1, nrow, CHUNK), lambda i, c: (i, c, 0, 0)),
            pl.BlockSpec((1, 1, nrow, CHUNK), lambda i, c: (i, bwd_chunk(c), 0, 0)),
        ],
        out_specs=[
            pl.BlockSpec((1, CHUNK, kw), lambda i, c: (i, c, 0)),
            pl.BlockSpec((1, CHUNK, kw), lambda i, c: (i, bwd_chunk(c), 0)),
        ],
        out_shape=[jax.ShapeDtypeStruct((bsz, lt, kw), F32), jax.ShapeDtypeStruct((bsz, lt, kw), F32)],
        scratch_shapes=[pltpu.VMEM((2 * GDN_HEADS, GDN_DK, GDN_DK), F32)],
        compiler_params=_params(("parallel", "arbitrary")),
        name="gdn_delta",
    )(qkv, qkv, qkv, qkv, qkv, qkv, col_pack, col_pack, row_pack, row_pack)


def _gdn_gates(p_small, a_log, dt_bias):
    bsz, lt, _ = p_small.shape
    nh = GDN_HEADS
    beta = jax.nn.sigmoid(p_small[..., :2 * nh])
    a = p_small[..., 2 * nh:4 * nh]
    g = -jnp.exp(a_log.reshape(2 * nh)) * jax.nn.softplus(a + dt_bias.reshape(2 * nh))
    gch = g.reshape(bsz, lt // CHUNK, CHUNK, 2 * nh)
    gf = jnp.cumsum(gch[..., :nh], axis=2)
    gb = jnp.flip(jnp.cumsum(jnp.flip(gch[..., nh:], axis=2), axis=2), axis=2)
    gc = jnp.concatenate([gf, gb], axis=-1)
    col_pack = jnp.concatenate([gc, beta.reshape(bsz, lt // CHUNK, CHUNK, 2 * nh)], axis=-1)
    row_pack = jnp.swapaxes(gc, 2, 3)
    return col_pack, row_pack


def _rope_tables(n_ctx, seq):
    half = AXIS_DIM // 2
    lane = jnp.arange(LANES)
    inv = ROPE_THETA ** (-jnp.arange(0, AXIS_DIM, 2, dtype=F32) / AXIS_DIM)
    freq = inv[lane % half]
    use_col = (lane % HEAD_DIM) // AXIS_DIM == 1
    sign = jnp.where((lane // half) % 2 == 0, -1.0, 1.0)
    tok = jnp.arange(seq)
    pos = jnp.where(use_col[None, :], (tok % GRID_W)[:, None], (tok // GRID_W)[:, None]).astype(F32)
    ang = pos * freq[None, :]
    cos = jnp.concatenate([jnp.ones((n_ctx, LANES), F32), jnp.cos(ang)], axis=0)
    sin = jnp.concatenate([jnp.zeros((n_ctx, LANES), F32), jnp.sin(ang) * sign[None, :]], axis=0)
    return cos, sin


def _pad_cols(a, n):
    return jnp.pad(a, [(0, 0)] * (a.ndim - 1) + [(0, n - a.shape[-1])])


def kernel(x, c, ctx, c_ctx, w_mod, b_mod, g_mix, g_ffn, win_w_qkv, win_b_qkv, win_sink, win_w_o, win_b_o, glb_w_qkv, glb_g_q, glb_g_k, glb_w_o, gdn_w_in, gdn_conv_w, gdn_a_log, gdn_dt_bias, gdn_g_out, gdn_w_o, moe_w_router, moe_b_router, moe_w_up, moe_b_up, moe_w_down, moe_b_down, g_final):
    bsz, seq, d = x.shape
    n_ctx = ctx.shape[1]
    depth = w_mod.shape[0]
    assert n_ctx % ROW_TILE == 0 and seq % ROW_TILE == 0 and n_ctx >= WINDOW and seq >= ATTN_TQ + 2 * WINDOW
    nq, nk = N_HEADS * HEAD_DIM, N_KV_HEADS * HEAD_DIM

    xs = jnp.concatenate([ctx, x], axis=1)
    cond_rows = -(-(bsz + 1) // 8) * 8
    cond = jnp.zeros((cond_rows, d), F32).at[:bsz].set(c).at[bsz].set(c_ctx)
    mods = _adaln(cond, w_mod, b_mod).reshape(depth, cond_rows, 6, d)
    cos, sin = _rope_tables(n_ctx, seq)
    zero_bias = jnp.zeros((1, d), F32)
    assert d == TOKEN_ROWS * LANES
    moe_buf = jnp.zeros((_moe_rows(bsz * (n_ctx + seq)) * TOKEN_ROWS, LANES), F32)

    for i in range(depth):
        kind, j = i % 3, i // 3
        mod = jnp.stack([jnp.broadcast_to(mods[i, bsz], (bsz, 6, d)), mods[i, :bsz]], axis=1)
        g1 = g_mix[i].reshape(1, d)
        w_r = _pad_cols(moe_w_router[i], LANES)
        w_r_hi = w_r.astype(BF16)
        w_r = jnp.stack([w_r_hi, (w_r - w_r_hi.astype(F32)).astype(BF16)])
        b_r = jnp.concatenate([moe_b_router[i], jnp.full((LANES - N_EXPERTS,), NEG_INF, F32)]).reshape(1, LANES)
        route_args = (g_ffn[i].reshape(1, d), w_r, b_r)
        if kind == 0:
            qkv = _qkv_project(xs, mod, g1, win_w_qkv[j].astype(BF16), win_b_qkv[j].reshape(1, -1), cos, sin,
                               jnp.ones((1, nq + nk), F32), n_ctx=n_ctx, qk_norm=False)
            o = _attention(qkv, win_sink[j], windowed=True, n_ctx=n_ctx)
            outs = _mixer_out((o,), xs, mod, win_w_o[j].astype(BF16), win_b_o[j].reshape(1, d), *route_args,
                              n_ctx=n_ctx)
        elif kind == 1:
            hg = jnp.concatenate([jnp.tile(glb_g_q[j], N_HEADS), jnp.tile(glb_g_k[j], N_KV_HEADS)]).reshape(1, -1)
            qkv = _qkv_project(xs, mod, g1, glb_w_qkv[j].astype(BF16), jnp.zeros((1, nq + 2 * nk), F32), cos, sin,
                               hg, n_ctx=n_ctx, qk_norm=True)
            o = _attention(qkv, jnp.zeros((N_HEADS,), F32), windowed=False, n_ctx=n_ctx)
            outs = _mixer_out((o,), xs, mod, glb_w_o[j].astype(BF16), zero_bias, *route_args, n_ctx=n_ctx)
        else:
            n_main = 2 * GDN_HEADS * GDN_DK + 2 * GDN_HEADS * LANES
            w_in = _pad_cols(gdn_w_in[j], n_main + LANES).astype(BF16)
            p_main, p_small = _gdn_in_project(xs, mod, g1, w_in, n_ctx=n_ctx, n_main=n_main)
            qkv = _gdn_conv(p_main, gdn_conv_w[j], n_ctx=n_ctx)
            col_pack, row_pack = _gdn_gates(p_small, gdn_a_log[j], gdn_dt_bias[j])
            o_f, o_b = _gdn_delta(qkv, col_pack, row_pack, n_ctx=n_ctx)
            outs = _mixer_out((o_f, o_b, p_main), xs, mod, gdn_w_o[j].astype(BF16), zero_bias, *route_args,
                              n_ctx=n_ctx, gdn_gain=gdn_g_out[j].reshape(1, LANES))
        x_mid, h2, idx, gates = outs
        xs, moe_buf = _moe_layer(x_mid, h2, idx, gates, mod, moe_w_up, moe_b_up[i], moe_w_down, moe_b_down[i],
                                 moe_buf, n_ctx=n_ctx, layer=i,
                                 final_gain=g_final.reshape(1, d) if i == depth - 1 else None)
    return xs
```

```python
import functools

import jax
import jax.numpy as jnp
from jax import lax
from jax.experimental import pallas as pl
from jax.experimental.pallas import tpu as pltpu

F32 = jnp.float32
BF16 = jnp.bfloat16

GRID_W = 64
HEAD_DIM = 64
N_HEADS = 16
N_KV_HEADS = 4
GROUP = N_HEADS // N_KV_HEADS
WINDOW = 128
ROPE_THETA = 10000.0
AXIS_DIM = HEAD_DIM // 2
GDN_HEADS = 8
GDN_DK = 128
CONV_K = 5
CHUNK = 64
GDN_BATCH = 2
N_EXPERTS = 32
TOP_K = 4
SWIGLU_LIMIT = 7.0
SWIGLU_ALPHA = 1.702
NORM_EPS = 1e-6
NEG_INF = -1e30

LANES = 128
TOKEN_ROWS = 8
ROW_TILE = 256
ATTN_TQ = 128
GLOBAL_TQ = 256
MIX_SUB = 128
MOE_TILE = 256
ADALN_COL_TILES = 4
CONV_HEADS_PER_TILE = 4
VMEM_LIMIT = 48 * 1024 * 1024
MOE_VMEM_LIMIT = 56 * 1024 * 1024


def _params(sem):
    return pltpu.CompilerParams(dimension_semantics=sem, vmem_limit_bytes=VMEM_LIMIT)


def _rms(x, eps=NORM_EPS):
    return x * lax.rsqrt(jnp.mean(x * x, axis=-1, keepdims=True) + eps)


def _dot(a, b):
    return jnp.dot(a, b, preferred_element_type=F32)


def _dot_nt(a, b):
    return lax.dot_general(a, b, (((1,), (1,)), ((), ())), preferred_element_type=F32)


def _dot_tn(a, b):
    return lax.dot_general(a, b, (((0,), (0,)), ((), ())), preferred_element_type=F32)


def _adaln_kernel(cond_ref, w_ref, b_ref, o_ref):
    c = cond_ref[...]
    a = (c * jax.nn.sigmoid(c)).astype(BF16)
    o_ref[0] = _dot(a, w_ref[0].astype(BF16)) + b_ref[0]


def _adaln(cond, w_mod, b_mod):
    depth, d, n = w_mod.shape
    r = cond.shape[0]
    tn = n // ADALN_COL_TILES
    return pl.pallas_call(
        _adaln_kernel,
        grid=(depth, n // tn),
        in_specs=[
            pl.BlockSpec((r, d), lambda i, j: (0, 0)),
            pl.BlockSpec((1, d, tn), lambda i, j: (i, 0, j)),
            pl.BlockSpec((1, 1, tn), lambda i, j: (i, 0, j)),
        ],
        out_specs=pl.BlockSpec((1, r, tn), lambda i, j: (i, 0, j)),
        out_shape=jax.ShapeDtypeStruct((depth, r, n), F32),
        compiler_params=_params(("parallel", "parallel")),
        name="adaln",
    )(cond, w_mod, b_mod.reshape(depth, 1, n))


def _norm_mod(x, m, g, which):
    return _rms(x) * g * (1.0 + m[which + 1:which + 2]) + m[which:which + 1]


def _swap16(t):
    lane = lax.broadcasted_iota(jnp.int32, t.shape, 1)
    first = (lane // (AXIS_DIM // 2)) % 2 == 0
    return jnp.where(first, pltpu.roll(t, LANES - AXIS_DIM // 2, 1), pltpu.roll(t, AXIS_DIM // 2, 1))


def _head_sum_matrix():
    r = lax.broadcasted_iota(jnp.int32, (LANES, LANES), 0) // HEAD_DIM
    c = lax.broadcasted_iota(jnp.int32, (LANES, LANES), 1) // HEAD_DIM
    return (r == c).astype(BF16)


def _qkv_kernel(x_ref, mod_ref, g_ref, w_ref, b_ref, cos_ref, sin_ref, hg_ref, o_ref, *, qk_norm):
    nq = N_HEADS * HEAD_DIM
    nk = N_KV_HEADS * HEAD_DIM
    h = _norm_mod(x_ref[0], mod_ref[0, 0], g_ref[...], 0)
    p = _dot(h.astype(BF16), w_ref[...]) + b_ref[...]
    cos = cos_ref[...]
    sin = sin_ref[...]
    lane = lax.broadcasted_iota(jnp.int32, cos.shape, 1)
    low = lane < HEAD_DIM
    if qk_norm:
        seg = _head_sum_matrix()

    def expand(t, base, ones_lane):
        r = pltpu.roll(t, HEAD_DIM, 1)
        fill_hi = jnp.where(lane == HEAD_DIM, 1.0, 0.0) if ones_lane else jnp.zeros_like(t)
        fill_lo = jnp.where(lane == 0, 1.0, 0.0) if ones_lane else jnp.zeros_like(t)
        parts = (jnp.where(low, t, fill_hi), jnp.where(low, fill_lo, r),
                 jnp.where(low, r, fill_hi), jnp.where(low, fill_lo, t))
        for i, part in enumerate(parts):
            o_ref[0, :, base + i * LANES: base + (i + 1) * LANES] = part.astype(o_ref.dtype)

    for j in range((nq + nk) // LANES):
        t = p[:, j * LANES:(j + 1) * LANES]
        if qk_norm:
            sq = t * t
            sq_hi = sq.astype(BF16)
            ss = _dot(sq_hi, seg) + _dot((sq - sq_hi.astype(F32)).astype(BF16), seg)
            t = t * lax.rsqrt(ss * (1.0 / HEAD_DIM) + NORM_EPS) * hg_ref[:, j * LANES:(j + 1) * LANES]
        t = t * cos + _swap16(t) * sin
        if j < nq // LANES:
            o_ref[0, :, j * LANES:(j + 1) * LANES] = (t * (HEAD_DIM ** -0.5)).astype(o_ref.dtype)
        else:
            expand(t, nq + (j - nq // LANES) * 4 * LANES, False)
    for j in range(nk // LANES):
        t = p[:, nq + nk + j * LANES: nq + nk + (j + 1) * LANES]
        expand(t, nq + 4 * nk + j * 4 * LANES, True)


def _qkv_project(x, mod, g, w, b, cos, sin, hg, *, n_ctx, qk_norm):
    bsz, lt, d = x.shape
    n = w.shape[1]
    tm = ROW_TILE
    nct = n_ctx // tm
    n_out = N_HEADS * HEAD_DIM + 8 * N_KV_HEADS * HEAD_DIM
    return pl.pallas_call(
        functools.partial(_qkv_kernel, qk_norm=qk_norm),
        grid=(bsz, lt // tm),
        in_specs=[
            pl.BlockSpec((1, tm, d), lambda i, t: (i, t, 0)),
            pl.BlockSpec((1, 1, 6, d), lambda i, t: (i, (t >= nct).astype(jnp.int32), 0, 0)),
            pl.BlockSpec((1, d), lambda i, t: (0, 0)),
            pl.BlockSpec((d, n), lambda i, t: (0, 0)),
            pl.BlockSpec((1, n), lambda i, t: (0, 0)),
            pl.BlockSpec((tm, LANES), lambda i, t: (t, 0)),
            pl.BlockSpec((tm, LANES), lambda i, t: (t, 0)),
            pl.BlockSpec((1, hg.shape[1]), lambda i, t: (0, 0)),
        ],
        out_specs=pl.BlockSpec((1, tm, n_out), lambda i, t: (i, t, 0)),
        out_shape=jax.ShapeDtypeStruct((bsz, lt, n_out), BF16),
        compiler_params=_params(("parallel", "parallel")),
        name="qkv_project",
    )(x, mod, g, w, b, cos, sin, hg)


def _gdn_in_kernel(x_ref, mod_ref, g_ref, w_ref, o_ref, o2_ref):
    h = _norm_mod(x_ref[0], mod_ref[0, 0], g_ref[...], 0)
    p = _dot(h.astype(BF16), w_ref[...])
    n_main = o_ref.shape[2]
    o_ref[0] = p[:, :n_main].astype(o_ref.dtype)
    o2_ref[0] = p[:, n_main:]


def _gdn_in_project(x, mod, g, w, *, n_ctx, n_main):
    bsz, lt, d = x.shape
    n = w.shape[1]
    tm = ROW_TILE
    nct = n_ctx // tm
    return pl.pallas_call(
        _gdn_in_kernel,
        grid=(bsz, lt // tm),
        in_specs=[
            pl.BlockSpec((1, tm, d), lambda i, t: (i, t, 0)),
            pl.BlockSpec((1, 1, 6, d), lambda i, t: (i, (t >= nct).astype(jnp.int32), 0, 0)),
            pl.BlockSpec((1, d), lambda i, t: (0, 0)),
            pl.BlockSpec((d, n), lambda i, t: (0, 0)),
        ],
        out_specs=[
            pl.BlockSpec((1, tm, n_main), lambda i, t: (i, t, 0)),
            pl.BlockSpec((1, tm, n - n_main), lambda i, t: (i, t, 0)),
        ],
        out_shape=[
            jax.ShapeDtypeStruct((bsz, lt, n_main), BF16),
            jax.ShapeDtypeStruct((bsz, lt, n - n_main), F32),
        ],
        compiler_params=_params(("parallel", "parallel")),
        name="gdn_in_project",
    )(x, mod, g, w)


def _attn_kernel(sink_ref, q_ref, k_ref, v_ref, o_ref, *, windowed, n_ctx, seq):
    tq = q_ref.shape[1]
    t = pl.program_id(1)
    nct = n_ctx // tq
    gw = 2 * LANES
    top = lax.broadcasted_iota(jnp.int32, (2 * tq, 1), 0) < tq
    low = lax.broadcasted_iota(jnp.int32, (2 * tq, LANES), 1) < HEAD_DIM

    def sink_rows(g, half):
        return jnp.where(top, sink_ref[g * GROUP + half], sink_ref[g * GROUP + 2 + half])

    def run(segments, batch):
        for g0 in range(0, N_KV_HEADS, batch):
            heads = range(g0, g0 + batch)
            jobs = [(g, half) for g in heads for half in range(2)]
            qs = {g: jnp.concatenate([q_ref[0, :, g * gw:g * gw + LANES],
                                      q_ref[0, :, g * gw + LANES:(g + 1) * gw]], axis=0) for g in heads}
            scores, ms, outs = {}, {}, {}
            for g, half in jobs:
                cols = slice(g * gw + half * LANES, g * gw + (half + 1) * LANES)
                scores[g, half] = []
                for rows, mask in segments:
                    s = _dot_nt(qs[g], k_ref[0, rows, cols])
                    if mask is not None:
                        s = jnp.where(mask, s, NEG_INF)
                    scores[g, half].append(s)
            for g, half in jobs:
                m = scores[g, half][0].max(axis=-1, keepdims=True)
                for s in scores[g, half][1:]:
                    m = jnp.maximum(m, s.max(axis=-1, keepdims=True))
                if windowed:
                    m = jnp.maximum(m, sink_rows(g, half))
                ms[g, half] = m
            for g, half in jobs:
                cols = slice(g * gw + half * LANES, g * gw + (half + 1) * LANES)
                o = None
                for s, (rows, _) in zip(scores[g, half], segments):
                    p = jnp.exp((s - ms[g, half]).astype(BF16))
                    pv = _dot(p, v_ref[0, rows, cols])
                    o = pv if o is None else o + pv
                outs[g, half] = o
            for g in heads:
                den_lo = outs[g, 0][:, HEAD_DIM:HEAD_DIM + 1]
                den_hi = outs[g, 1][:, 0:1]
                if windowed:
                    den_lo = den_lo + jnp.exp(sink_rows(g, 0) - ms[g, 0])
                    den_hi = den_hi + jnp.exp(sink_rows(g, 1) - ms[g, 1])
                acc = jnp.where(low, outs[g, 0] * (1.0 / den_lo), outs[g, 1] * (1.0 / den_hi))
                o_ref[0, :, g * gw:g * gw + LANES] = acc[:tq].astype(o_ref.dtype)
                o_ref[0, :, g * gw + LANES:(g + 1) * gw] = acc[tq:].astype(o_ref.dtype)

    ctx_rows = slice(0, n_ctx)

    @pl.when(t < nct)
    def _():
        run([(ctx_rows, None)], N_KV_HEADS)

    @pl.when(t >= nct)
    def _():
        if windowed:
            span = tq + 2 * WINDOW
            start = (t - nct) * tq
            r0 = pl.multiple_of(jnp.minimum(n_ctx + start - WINDOW, n_ctx + seq - span), LANES)
            qpos = start + lax.broadcasted_iota(jnp.int32, (tq, span), 0)
            kpos = r0 - n_ctx + lax.broadcasted_iota(jnp.int32, (tq, span), 1)
            band = (jnp.abs(qpos - kpos) <= WINDOW) & (kpos >= 0)
            band = jnp.concatenate([band, band], axis=0)
            run([(pl.ds(r0, span), band), (ctx_rows, None)], N_KV_HEADS)
        else:
            run([(slice(0, n_ctx + seq), None)], 1)


def _attention(qkv, sink, *, windowed, n_ctx):
    bsz, lt, _ = qkv.shape
    tq = ATTN_TQ if windowed else GLOBAL_TQ
    nq = N_HEADS * HEAD_DIM
    kernel = functools.partial(_attn_kernel, windowed=windowed, n_ctx=n_ctx, seq=lt - n_ctx)
    return pl.pallas_call(
        kernel,
        grid=(bsz, lt // tq),
        in_specs=[
            pl.BlockSpec(memory_space=pltpu.SMEM),
            pl.BlockSpec((1, tq, nq), lambda i, t: (i, t, 0)),
            pl.BlockSpec((1, lt, nq), lambda i, t: (i, 0, 1)),
            pl.BlockSpec((1, lt, nq), lambda i, t: (i, 0, 2)),
        ],
        out_specs=pl.BlockSpec((1, tq, nq), lambda i, t: (i, t, 0)),
        out_shape=jax.ShapeDtypeStruct((bsz, lt, nq), BF16),
        compiler_params=_params(("parallel", "arbitrary")),
        name="window_attention" if windowed else "global_attention",
    )(sink, qkv, qkv, qkv)


def _sub_tiles(tm):
    return [slice(r, r + MIX_SUB) for r in range(0, tm, MIX_SUB)]


def _mix_out_tail(ys, x_ref, m, gffn_ref, wr_ref, br_ref, xmid_ref, h2_ref, idx_ref, gate_ref):
    g = gffn_ref[...]
    w_hi, w_lo, b_r = wr_ref[0], wr_ref[1], br_ref[...]
    hs = []
    for rows, y in ys:
        xm = x_ref[0, rows, :] + m[2:3] * y
        xmid_ref[0, rows, :] = xm
        h = _norm_mod(xm, m, g, 3)
        for s in range(TOKEN_ROWS):
            h2_ref[0, pl.ds(rows.start * TOKEN_ROWS + s, rows.stop - rows.start, stride=TOKEN_ROWS), :] = (
                h[:, s * LANES:(s + 1) * LANES])
        hs.append(h)
    logits = []
    for h in hs:
        h_hi = h.astype(BF16)
        h_lo = (h - h_hi.astype(F32)).astype(BF16)
        logits.append(_dot(h_hi, w_hi) + (_dot(h_hi, w_lo) + _dot(h_lo, w_hi)) + b_r)
    lane = lax.broadcasted_iota(jnp.int32, logits[0].shape, 1)
    vals = [[] for _ in ys]
    idxs = [[] for _ in ys]
    for _ in range(TOP_K):
        for s in range(len(ys)):
            mx = logits[s].max(axis=-1, keepdims=True)
            ix = jnp.where(logits[s] == mx, lane, LANES).min(axis=-1, keepdims=True)
            vals[s].append(mx)
            idxs[s].append(ix)
            logits[s] = jnp.where(lane == ix, -jnp.inf, logits[s])
    for s, (rows, _) in enumerate(ys):
        es = [jnp.exp(v - vals[s][0]) for v in vals[s]]
        inv = 1.0 / (es[0] + es[1] + es[2] + es[3])
        gates = jnp.zeros(lane.shape, F32)
        idx = jnp.zeros(lane.shape, jnp.int32)
        for k in range(TOP_K):
            gates = jnp.where(lane == k, es[k] * inv, gates)
            idx = jnp.where(lane == k, idxs[s][k], idx)
        idx_ref[0, rows, :] = idx
        gate_ref[0, rows, :] = gates


def _attn_out_kernel(o_ref, x_ref, mod_ref, wo_ref, bo_ref, gffn_ref, wr_ref, br_ref,
                     xmid_ref, h2_ref, idx_ref, gate_ref):
    ys = [(rows, _dot(o_ref[0, rows, :], wo_ref[...]) + bo_ref[...]) for rows in _sub_tiles(o_ref.shape[1])]
    _mix_out_tail(ys, x_ref, mod_ref[0, 0], gffn_ref, wr_ref, br_ref, xmid_ref, h2_ref, idx_ref, gate_ref)


def _gdn_out_kernel(of_ref, ob_ref, z_ref, gout_ref, x_ref, mod_ref, wo_ref, bo_ref, gffn_ref, wr_ref, br_ref,
                    xmid_ref, h2_ref, idx_ref, gate_ref):
    gout = gout_ref[...]
    ys = []
    for rows in _sub_tiles(of_ref.shape[1]):
        parts = []
        for hd in range(GDN_HEADS):
            cols = slice(hd * LANES, (hd + 1) * LANES)
            z = z_ref[0, rows, cols].astype(F32)
            o = _rms(of_ref[0, rows, cols] + ob_ref[0, rows, cols]) * gout
            parts.append((o * (z * jax.nn.sigmoid(z))).astype(BF16))
        ys.append((rows, _dot(jnp.concatenate(parts, axis=1), wo_ref[...]) + bo_ref[...]))
    _mix_out_tail(ys, x_ref, mod_ref[0, 0], gffn_ref, wr_ref, br_ref, xmid_ref, h2_ref, idx_ref, gate_ref)


def _mixer_out(o_inputs, x, mod, w_o, b_o, g_ffn, w_r, b_r, *, n_ctx, gdn_gain=None):
    bsz, lt, d = x.shape
    tm = ROW_TILE
    nct = n_ctx // tm
    row = lambda i, t: (i, t, 0)
    const = lambda i, t: (0, 0)
    if gdn_gain is None:
        kernel = _attn_out_kernel
        head_specs = [pl.BlockSpec((1, tm, d), row)]
        head_args = list(o_inputs)
    else:
        kernel = _gdn_out_kernel
        o_f, o_b, p_main = o_inputs
        zblk = p_main.shape[2] // d - 1
        head_specs = [pl.BlockSpec((1, tm, d), row), pl.BlockSpec((1, tm, d), row),
                      pl.BlockSpec((1, tm, d), lambda i, t: (i, t, zblk)),
                      pl.BlockSpec((1, LANES), const)]
        head_args = [o_f, o_b, p_main, gdn_gain]
    return pl.pallas_call(
        kernel,
        grid=(bsz, lt // tm),
        in_specs=head_specs + [
            pl.BlockSpec((1, tm, d), row),
            pl.BlockSpec((1, 1, 6, d), lambda i, t: (i, (t >= nct).astype(jnp.int32), 0, 0)),
            pl.BlockSpec((d, d), const),
            pl.BlockSpec((1, d), const),
            pl.BlockSpec((1, d), const),
            pl.BlockSpec((2, d, LANES), lambda i, t: (0, 0, 0)),
            pl.BlockSpec((1, LANES), const),
        ],
        out_specs=[
            pl.BlockSpec((1, tm, d), row),
            pl.BlockSpec((1, tm * TOKEN_ROWS, LANES), row),
            pl.BlockSpec((1, tm, LANES), row),
            pl.BlockSpec((1, tm, LANES), row),
        ],
        out_shape=[
            jax.ShapeDtypeStruct((bsz, lt, d), F32),
            jax.ShapeDtypeStruct((bsz, lt * TOKEN_ROWS, LANES), F32),
            jax.ShapeDtypeStruct((bsz, lt, LANES), jnp.int32),
            jax.ShapeDtypeStruct((bsz, lt, LANES), F32),
        ],
        compiler_params=_params(("parallel", "parallel")),
        name="mixer_out_router",
    )(*head_args, x, mod, w_o, b_o, g_ffn, w_r, b_r)


def _route_rank_kernel(idx_ref, rank_ref, cnt_ref, run_ref):
    i = pl.program_id(0)

    @pl.when(i == 0)
    def _():
        run_ref[...] = jnp.zeros_like(run_ref)

    idx = idx_ref[...]
    tm = idx.shape[0]
    lane = lax.broadcasted_iota(jnp.int32, idx.shape, 1)
    earlier = (lax.broadcasted_iota(jnp.int32, (tm, tm), 1) < lax.broadcasted_iota(jnp.int32, (tm, tm), 0)).astype(BF16)
    base = run_ref[...]
    rank = jnp.zeros(idx.shape, jnp.int32)
    for k in range(TOP_K):
        hit = lane == idx[:, k:k + 1]
        before = _dot(earlier, hit.astype(BF16)) + base
        r = jnp.sum(jnp.where(hit, before, 0.0), axis=-1, keepdims=True)
        rank = jnp.where(lane == k, r.astype(jnp.int32), rank)
        base = base + jnp.sum(hit.astype(F32), axis=0, keepdims=True)
    rank_ref[...] = rank
    run_ref[...] = base
    cnt_ref[...] = jnp.broadcast_to(base, cnt_ref.shape)


def _route_rank(idx):
    n = idx.shape[0]
    tm = 2 * ROW_TILE if n % (2 * ROW_TILE) == 0 else ROW_TILE
    return pl.pallas_call(
        _route_rank_kernel,
        grid=(n // tm,),
        in_specs=[pl.BlockSpec((tm, LANES), lambda i: (i, 0))],
        out_specs=[pl.BlockSpec((tm, LANES), lambda i: (i, 0)), pl.BlockSpec((8, LANES), lambda i: (0, 0))],
        out_shape=[jax.ShapeDtypeStruct((n, LANES), jnp.int32), jax.ShapeDtypeStruct((8, LANES), F32)],
        scratch_shapes=[pltpu.VMEM((1, LANES), F32)],
        compiler_params=_params(("arbitrary",)),
        name="route_rank",
    )(idx)


def _dispatch_kernel(dest_ref, h_ref, xs_in_ref, xs_ref, sem):
    del xs_in_ref
    tm = h_ref.shape[0] // TOKEN_ROWS

    def row_copy(r, dst):
        src = h_ref.at[pl.ds(pl.multiple_of(r * TOKEN_ROWS, TOKEN_ROWS), TOKEN_ROWS)]
        return pltpu.make_async_copy(src, xs_ref.at[pl.ds(pl.multiple_of(dst * TOKEN_ROWS, TOKEN_ROWS), TOKEN_ROWS)], sem)

    def issue(r, carry):
        for k in range(TOP_K):
            row_copy(r, dest_ref[0, 0, r * TOP_K + k]).start(priority=k % 2)
        return carry

    lax.fori_loop(0, tm, issue, 0, unroll=8)
    def drain(r, carry):
        for k in range(TOP_K):
            row_copy(r, 0).wait()
        return carry

    lax.fori_loop(0, tm, drain, 0, unroll=8)


def _dispatch(h2, dest, xs_buf):
    n = dest.shape[0]
    tm = ROW_TILE
    dest3 = dest.reshape(n // tm, 1, tm * TOP_K)
    return pl.pallas_call(
        _dispatch_kernel,
        grid=(n // tm,),
        in_specs=[
            pl.BlockSpec((1, 1, tm * TOP_K), lambda i: (i, 0, 0), memory_space=pltpu.SMEM),
            pl.BlockSpec((tm * TOKEN_ROWS, LANES), lambda i: (i, 0)),
            pl.BlockSpec(memory_space=pl.ANY),
        ],
        out_specs=pl.BlockSpec(memory_space=pl.ANY),
        out_shape=jax.ShapeDtypeStruct(xs_buf.shape, h2.dtype),
        scratch_shapes=[pltpu.SemaphoreType.DMA],
        input_output_aliases={2: 0},
        compiler_params=_params(("arbitrary",)),
        name="moe_dispatch",
    )(dest3, h2, xs_buf)


def _moe_kernel(te_ref, nu_ref, first_ref, nxt_ref, slot_ref, x_ref, bu_ref, bd_ref, wu_hbm, wd_hbm, o_ref,
                wu_f32, wd_f32, wu16, wd16, sem, *, layer):
    i = pl.program_id(0)
    de = wd16.shape[0]

    def weight_copies(expert, slot):
        return (pltpu.make_async_copy(wu_hbm.at[layer, expert], wu_f32.at[slot], sem.at[0, slot]),
                pltpu.make_async_copy(wd_hbm.at[layer, expert], wd_f32.at[slot], sem.at[1, slot]))

    active = i < nu_ref[0]

    @pl.when(active & (i == 0))
    def _():
        for cp in weight_copies(te_ref[0], 0):
            cp.start()

    @pl.when(active & (first_ref[i] == 1))
    def _():
        slot = slot_ref[i]
        for cp in weight_copies(te_ref[i], slot):
            cp.wait()

        @pl.when(nxt_ref[i] >= 0)
        def _():
            for cp in weight_copies(nxt_ref[i], 1 - slot):
                cp.start()

        wu16[...] = wu_f32[slot].astype(BF16)
        wd16[...] = wd_f32[slot].astype(BF16)

    @pl.when(active)
    def _():
        tm = o_ref.shape[0]
        x = jnp.concatenate([x_ref[pl.ds(s, tm, stride=TOKEN_ROWS), :].astype(BF16) for s in range(TOKEN_ROWS)], axis=1)
        u = _dot(x, wu16[...]) + bu_ref[0]
        glu = jnp.minimum(u[:, :de], SWIGLU_LIMIT)
        lin = jnp.clip(u[:, de:], -SWIGLU_LIMIT, SWIGLU_LIMIT)
        act = glu * jax.nn.sigmoid(SWIGLU_ALPHA * glu) * (lin + 1.0)
        o_ref[...] = (_dot(act.astype(BF16), wd16[...]) + bd_ref[0]).astype(o_ref.dtype)

    @pl.when(jnp.logical_not(active))
    def _():
        o_ref[...] = jnp.zeros_like(o_ref)


def _moe_experts(xs, tile_expert, n_used, first, nxt, slot, w_up, b_up, w_down, b_down, *, layer):
    rows = xs.shape[0] // TOKEN_ROWS
    tm = MOE_TILE
    n_tiles = rows // tm
    _, ne, d, n_up = w_up.shape
    de = w_down.shape[2]
    grid_spec = pltpu.PrefetchScalarGridSpec(
        num_scalar_prefetch=5,
        grid=(n_tiles,),
        in_specs=[
            pl.BlockSpec((tm * TOKEN_ROWS, LANES), lambda i, te, nu, *_: (jnp.minimum(i, nu[0] - 1), 0)),
            pl.BlockSpec((1, 1, n_up), lambda i, te, *_: (te[i], 0, 0)),
            pl.BlockSpec((1, 1, d), lambda i, te, *_: (te[i], 0, 0)),
            pl.BlockSpec(memory_space=pl.ANY),
            pl.BlockSpec(memory_space=pl.ANY),
        ],
        out_specs=pl.BlockSpec((tm, d), lambda i, *_: (i, 0)),
        scratch_shapes=[
            pltpu.VMEM((2, d, n_up), F32),
            pltpu.VMEM((2, de, d), F32),
            pltpu.VMEM((d, n_up), BF16),
            pltpu.VMEM((de, d), BF16),
            pltpu.SemaphoreType.DMA((2, 2)),
        ],
    )
    return pl.pallas_call(
        functools.partial(_moe_kernel, layer=layer),
        grid_spec=grid_spec,
        out_shape=jax.ShapeDtypeStruct((rows, d), BF16),
        compiler_params=pltpu.CompilerParams(dimension_semantics=("arbitrary",), vmem_limit_bytes=MOE_VMEM_LIMIT),
        name="moe_experts",
    )(tile_expert, n_used, first, nxt, slot, xs, b_up.reshape(ne, 1, n_up), b_down.reshape(ne, 1, d), w_up, w_down)


def _combine_kernel(x_ref, y0_ref, y1_ref, y2_ref, y3_ref, gate_ref, mod_ref, gfin_ref, o_ref, *, final):
    gates = gate_ref[0]
    acc = gates[:, 0:1] * y0_ref[...].astype(F32)
    for k, y_ref in ((1, y1_ref), (2, y2_ref), (3, y3_ref)):
        acc = acc + gates[:, k:k + 1] * y_ref[...].astype(F32)
    x_new = x_ref[0] + mod_ref[0, 0][5:6] * acc
    o_ref[0] = _rms(x_new) * gfin_ref[...] if final else x_new


def _moe_combine(x, yg, gates, mod, *, n_ctx, final_gain=None):
    bsz, lt, d = x.shape
    tm = ROW_TILE
    nct = n_ctx // tm
    nt = lt // tm
    final = final_gain is not None
    skip = nct if final else 0
    row = lambda i, t: (i, t + skip, 0)
    slot = lambda k: pl.BlockSpec((tm, d), lambda i, t: (k * bsz * nt + i * nt + t + skip, 0))
    return pl.pallas_call(
        functools.partial(_combine_kernel, final=final),
        grid=(bsz, nt - skip),
        in_specs=[
            pl.BlockSpec((1, tm, d), row),
            slot(0), slot(1), slot(2), slot(3),
            pl.BlockSpec((1, tm, LANES), row),
            pl.BlockSpec((1, 1, 6, d), lambda i, t: (i, (t + skip >= nct).astype(jnp.int32), 0, 0)),
            pl.BlockSpec((1, d), lambda i, t: (0, 0)),
        ],
        out_specs=pl.BlockSpec((1, tm, d), lambda i, t: (i, t, 0)),
        out_shape=jax.ShapeDtypeStruct((bsz, lt - skip * tm, d), F32),
        compiler_params=_params(("parallel", "parallel")),
        name="moe_combine",
    )(x, yg, yg, yg, yg, gates, mod, final_gain if final else jnp.ones((1, d), F32))


def _moe_rows(n):
    return (-(-(n * TOP_K) // MOE_TILE) + N_EXPERTS) * MOE_TILE


def _moe_layer(x_mid, h2, idx, gates, mod, w_up, b_up, w_down, b_down, xs_buf, *, n_ctx, layer, final_gain=None):
    bsz, lt, d = x_mid.shape
    n = bsz * lt
    tm = MOE_TILE
    n_tiles = xs_buf.shape[0] // (tm * TOKEN_ROWS)
    idx2 = idx.reshape(n, LANES)
    rank, cnt = _route_rank(idx2)
    counts = cnt[0, :N_EXPERTS].astype(jnp.int32)
    padded = (counts + tm - 1) // tm * tm
    pend = jnp.cumsum(padded)
    pstart = pend - padded
    experts = idx2[:, :TOP_K]
    onehot = experts[:, :, None] == jnp.arange(N_EXPERTS, dtype=jnp.int32)[None, None, :]
    dest = jnp.sum(jnp.where(onehot, pstart[None, None, :], 0), axis=-1) + rank[:, :TOP_K]
    n_used = (pend[-1] // tm).astype(jnp.int32)
    tile_start = jnp.arange(n_tiles, dtype=jnp.int32) * tm
    tile_e = jnp.sum((tile_start[:, None] >= pend[None, :]).astype(jnp.int32), axis=1)
    last_e = jnp.sum((jnp.maximum(n_used - 1, 0) * tm >= pend).astype(jnp.int32))
    tile_e = jnp.minimum(jnp.where(jnp.arange(n_tiles) < n_used, tile_e, last_e), N_EXPERTS - 1)
    prev_e = jnp.concatenate([jnp.full((1,), -1, jnp.int32), tile_e[:-1]])
    first = (tile_e != prev_e).astype(jnp.int32)
    slot = (jnp.cumsum(first) - 1) % 2
    ids = jnp.arange(N_EXPERTS, dtype=jnp.int32)
    later_used = (ids[None, :] > ids[:, None]) & (counts[None, :] > 0)
    next_used = jnp.min(jnp.where(later_used, ids[None, :], N_EXPERTS), axis=1)
    next_used = jnp.where(next_used == N_EXPERTS, -1, next_used)
    nxt = jnp.sum(jnp.where(tile_e[:, None] == ids[None, :], next_used[None, :], 0), axis=1)
    xs = _dispatch(h2.reshape(n * TOKEN_ROWS, LANES), dest, xs_buf)
    ys = _moe_experts(xs, tile_e, n_used.reshape(1), first, nxt, slot, w_up, b_up, w_down, b_down, layer=layer)
    yg = ys.at[dest.T.reshape(-1)].get(mode='promise_in_bounds')
    return _moe_combine(x_mid, yg, gates, mod, n_ctx=n_ctx, final_gain=final_gain), xs


def _gdn_conv_kernel(p_ref, w_ref, o_ref, *, n_ctx):
    c = pl.program_id(1)
    lt, tc = p_ref.shape[1], p_ref.shape[2]
    x = p_ref[0].astype(F32)
    row = lax.broadcasted_iota(jnp.int32, (lt, 1), 0)
    is_ctx = row < n_ctx
    pos = jnp.where(is_ctx, row, row - n_ctx)
    seg_len = jnp.where(is_ctx, n_ctx, lt - n_ctx)
    acc = x * w_ref[CONV_K // 2:CONV_K // 2 + 1, :]
    for j in range(CONV_K):
        dlt = j - CONV_K // 2
        if dlt == 0:
            continue
        shifted = pltpu.roll(x, (-dlt) % lt, 0)
        valid = (pos + dlt >= 0) & (pos + dlt < seg_len)
        acc = acc + jnp.where(valid, shifted, 0.0) * w_ref[j:j + 1, :]
    y = acc * jax.nn.sigmoid(acc)
    n_qk_tiles = 2 * GDN_HEADS * GDN_DK // tc
    is_qk = c < n_qk_tiles
    scale = jnp.where(c < n_qk_tiles // 2, GDN_DK ** -0.5, 1.0)
    for hd in range(tc // LANES):
        t = y[:, hd * LANES:(hd + 1) * LANES]
        normed = t * lax.rsqrt(jnp.sum(t * t, axis=-1, keepdims=True) + NORM_EPS) * scale
        o_ref[0, :, hd * LANES:(hd + 1) * LANES] = jnp.where(is_qk, normed, t).astype(o_ref.dtype)


def _gdn_conv(p_main, conv_w, *, n_ctx):
    bsz, lt, _ = p_main.shape
    n = conv_w.shape[1]
    tc = CONV_HEADS_PER_TILE * LANES
    return pl.pallas_call(
        functools.partial(_gdn_conv_kernel, n_ctx=n_ctx),
        grid=(bsz, n // tc),
        in_specs=[
            pl.BlockSpec((1, lt, tc), lambda i, c: (i, 0, c)),
            pl.BlockSpec((CONV_K, tc), lambda i, c: (0, c)),
        ],
        out_specs=pl.BlockSpec((1, lt, tc), lambda i, c: (i, 0, c)),
        out_shape=jax.ShapeDtypeStruct((bsz, lt, n), BF16),
        compiler_params=_params(("parallel", "parallel")),
        name="gdn_conv",
    )(p_main, conv_w)


def _gdn_delta_kernel(qf_ref, kf_ref, vf_ref, qb_ref, kb_ref, vb_ref, colf_ref, colb_ref, rowf_ref, rowb_ref,
                      of_ref, ob_ref, s_ref):
    step = pl.program_id(1)

    @pl.when(step == 0)
    def _():
        s_ref[...] = jnp.zeros_like(s_ref)

    ii = lax.broadcasted_iota(jnp.int32, (CHUNK, CHUNK), 0)
    jj = lax.broadcasted_iota(jnp.int32, (CHUNK, CHUNK), 1)
    eye = (ii == jj).astype(F32)
    dirs = (
        (0, qf_ref, kf_ref, vf_ref, colf_ref, rowf_ref, of_ref, ii >= jj, ii > jj, CHUNK - 1),
        (1, qb_ref, kb_ref, vb_ref, colb_ref, rowb_ref, ob_ref, ii <= jj, ii < jj, 0),
    )
    chains = []
    for bb in range(qf_ref.shape[0]):
        for d, q_ref, k_ref, v_ref, col_ref, row_ref, o_ref, tri, strict, last in dirs:
            col = col_ref[bb, 0]
            rows = row_ref[bb, 0]
            for hd in range(GDN_HEADS):
                gi = d * GDN_HEADS + hd
                chains.append(dict(
                    bb=bb, gi=gi, si=bb * 2 * GDN_HEADS + gi, cols=slice(hd * LANES, (hd + 1) * LANES),
                    q_ref=q_ref, k_ref=k_ref, v_ref=v_ref, o_ref=o_ref, tri=tri, strict=strict, last=last,
                    gc=col[:, gi:gi + 1], beta=col[:, 2 * GDN_HEADS + gi:2 * GDN_HEADS + gi + 1],
                    gr=rows[gi:gi + 1, :]))

    for c in chains:
        k = c['k_ref'][c['bb'], :, c['cols']]
        c['k16'] = k
        c['kbeta'] = k.astype(F32) * c['beta']
        c['decay'] = jnp.where(c['tri'], jnp.exp(jnp.where(c['tri'], c['gc'] - c['gr'], 0.0)), 0.0)
    for c in chains:
        c['p'] = jnp.where(c['strict'], _dot_nt(c['kbeta'].astype(BF16), c['k16']) * c['decay'], 0.0)
        c['x'] = eye - c['p']
    for _ in range((CHUNK - 1).bit_length() - 1):
        for c in chains:
            p16 = c['p'].astype(BF16)
            c['p'] = _dot(p16, p16)
        for c in chains:
            c['x'] = c['x'] + _dot(c['x'].astype(BF16), c['p'].astype(BF16))
    for c in chains:
        t_inv = c['x'].astype(BF16)
        v = c['v_ref'][c['bb'], :, c['cols']].astype(F32)
        c['u'] = _dot(t_inv, (v * c['beta']).astype(BF16))
        c['w'] = _dot(t_inv, (c['kbeta'] * jnp.exp(c['gc'])).astype(BF16))
    for c in chains:
        q = c['q_ref'][c['bb'], :, c['cols']]
        c['intra'] = (_dot_nt(q, c['k16']) * c['decay']).astype(BF16)
        c['qg'] = (q.astype(F32) * jnp.exp(c['gc'])).astype(BF16)
    for c in chains:
        c['s'] = s_ref[c['si']]
        c['s16'] = c['s'].astype(BF16)
        c['v_new'] = (c['u'] - _dot(c['w'].astype(BF16), c['s16'])).astype(BF16)
    for c in chains:
        c['o_ref'][c['bb'], :, c['cols']] = _dot(c['qg'], c['s16']) + _dot(c['intra'], c['v_new'])
    for c in chains:
        g_last = c['gc'][c['last']:c['last'] + 1, :]
        k_dec = (c['k16'].astype(F32) * jnp.exp(g_last - c['gc'])).astype(BF16)
        s_ref[c['si']] = c['s'] * jnp.exp(g_last) + _dot_tn(k_dec, c['v_new'])


def _gdn_delta(qkv, col_pack, row_pack, *, n_ctx):
    bsz, lt, _ = qkv.shape
    nc = lt // CHUNK
    ncc = n_ctx // CHUNK
    kw = GDN_HEADS * GDN_DK

    def bwd_chunk(c):
        return jnp.where(c < ncc, ncc - 1 - c, nc - 1 - (c - ncc))

    nb = GDN_BATCH if bsz % GDN_BATCH == 0 else 1
    fwd = lambda blk: pl.BlockSpec((nb, CHUNK, kw), lambda i, c: (i, c, blk))
    bwd = lambda blk: pl.BlockSpec((nb, CHUNK, kw), lambda i, c: (i, bwd_chunk(c), blk))
    ncol, nrow = col_pack.shape[3], row_pack.shape[2]
    return pl.pallas_call(
        _gdn_delta_kernel,
        grid=(bsz // nb, nc),
        in_specs=[
            fwd(0), fwd(1), fwd(2), bwd(0), bwd(1), bwd(2),
            pl.BlockSpec((nb, 1, CHUNK, ncol), lambda i, c: (i, c, 0, 0)),
            pl.BlockSpec((nb, 1, CHUNK, ncol), lambda i, c: (i, bwd_chunk(c), 0, 0)),
            pl.BlockSpec((nb, 1, nrow, CHUNK), lambda i, c: (i, c, 0, 0)),
            pl.BlockSpec((nb, 1, nrow, CHUNK), lambda i, c: (i, bwd_chunk(c), 0, 0)),
        ],
        out_specs=[
            pl.BlockSpec((nb, CHUNK, kw), lambda i, c: (i, c, 0)),
            pl.BlockSpec((nb, CHUNK, kw), lambda i, c: (i, bwd_chunk(c), 0)),
        ],
        out_shape=[jax.ShapeDtypeStruct((bsz, lt, kw), F32), jax.ShapeDtypeStruct((bsz, lt, kw), F32)],
        scratch_shapes=[pltpu.VMEM((nb * 2 * GDN_HEADS, GDN_DK, GDN_DK), F32)],
        compiler_params=_params(("parallel", "arbitrary")),
        name="gdn_delta",
    )(qkv, qkv, qkv, qkv, qkv, qkv, col_pack, col_pack, row_pack, row_pack)


def _gdn_gates(p_small, a_log, dt_bias):
    bsz, lt, _ = p_small.shape
    nh = GDN_HEADS
    beta = jax.nn.sigmoid(p_small[..., :2 * nh])
    a = p_small[..., 2 * nh:4 * nh]
    g = -jnp.exp(a_log.reshape(2 * nh)) * jax.nn.softplus(a + dt_bias.reshape(2 * nh))
    gch = g.reshape(bsz, lt // CHUNK, CHUNK, 2 * nh)
    gf = jnp.cumsum(gch[..., :nh], axis=2)
    gb = jnp.flip(jnp.cumsum(jnp.flip(gch[..., nh:], axis=2), axis=2), axis=2)
    gc = jnp.concatenate([gf, gb], axis=-1)
    col_pack = jnp.concatenate([gc, beta.reshape(bsz, lt // CHUNK, CHUNK, 2 * nh)], axis=-1)
    row_pack = jnp.swapaxes(gc, 2, 3)
    return col_pack, row_pack


def _rope_tables(n_ctx, seq):
    half = AXIS_DIM // 2
    lane = jnp.arange(LANES)
    inv = ROPE_THETA ** (-jnp.arange(0, AXIS_DIM, 2, dtype=F32) / AXIS_DIM)
    freq = inv[lane % half]
    use_col = (lane % HEAD_DIM) // AXIS_DIM == 1
    sign = jnp.where((lane // half) % 2 == 0, -1.0, 1.0)
    tok = jnp.arange(seq)
    pos = jnp.where(use_col[None, :], (tok % GRID_W)[:, None], (tok // GRID_W)[:, None]).astype(F32)
    ang = pos * freq[None, :]
    cos = jnp.concatenate([jnp.ones((n_ctx, LANES), F32), jnp.cos(ang)], axis=0)
    sin = jnp.concatenate([jnp.zeros((n_ctx, LANES), F32), jnp.sin(ang) * sign[None, :]], axis=0)
    return cos, sin


def _pad_cols(a, n):
    return jnp.pad(a, [(0, 0)] * (a.ndim - 1) + [(0, n - a.shape[-1])])


def kernel(x, c, ctx, c_ctx, w_mod, b_mod, g_mix, g_ffn, win_w_qkv, win_b_qkv, win_sink, win_w_o, win_b_o, glb_w_qkv, glb_g_q, glb_g_k, glb_w_o, gdn_w_in, gdn_conv_w, gdn_a_log, gdn_dt_bias, gdn_g_out, gdn_w_o, moe_w_router, moe_b_router, moe_w_up, moe_b_up, moe_w_down, moe_b_down, g_final):
    bsz, seq, d = x.shape
    n_ctx = ctx.shape[1]
    depth = w_mod.shape[0]
    assert n_ctx % ROW_TILE == 0 and seq % ROW_TILE == 0 and n_ctx >= WINDOW and seq >= ATTN_TQ + 2 * WINDOW
    nq, nk = N_HEADS * HEAD_DIM, N_KV_HEADS * HEAD_DIM

    xs = jnp.concatenate([ctx, x], axis=1)
    cond_rows = -(-(bsz + 1) // TOKEN_ROWS) * TOKEN_ROWS
    cond = jnp.zeros((cond_rows, d), F32).at[:bsz].set(c).at[bsz].set(c_ctx)
    mods = _adaln(cond, w_mod, b_mod).reshape(depth, cond_rows, 6, d)
    cos, sin = _rope_tables(n_ctx, seq)
    zero_bias = jnp.zeros((1, d), F32)
    assert d == TOKEN_ROWS * LANES
    moe_buf = jnp.zeros((_moe_rows(bsz * (n_ctx + seq)) * TOKEN_ROWS, LANES), F32)

    for i in range(depth):
        kind, j = i % 3, i // 3
        mod = jnp.stack([jnp.broadcast_to(mods[i, bsz], (bsz, 6, d)), mods[i, :bsz]], axis=1)
        g1 = g_mix[i].reshape(1, d)
        w_r = _pad_cols(moe_w_router[i], LANES)
        w_r_hi = w_r.astype(BF16)
        w_r = jnp.stack([w_r_hi, (w_r - w_r_hi.astype(F32)).astype(BF16)])
        b_r = jnp.concatenate([moe_b_router[i], jnp.full((LANES - N_EXPERTS,), NEG_INF, F32)]).reshape(1, LANES)
        route_args = (g_ffn[i].reshape(1, d), w_r, b_r)
        if kind == 0:
            qkv = _qkv_project(xs, mod, g1, win_w_qkv[j].astype(BF16), win_b_qkv[j].reshape(1, -1), cos, sin,
                               jnp.ones((1, nq + nk), F32), n_ctx=n_ctx, qk_norm=False)
            o = _attention(qkv, win_sink[j], windowed=True, n_ctx=n_ctx)
            outs = _mixer_out((o,), xs, mod, win_w_o[j].astype(BF16), win_b_o[j].reshape(1, d), *route_args,
                              n_ctx=n_ctx)
        elif kind == 1:
            hg = jnp.concatenate([jnp.tile(glb_g_q[j], N_HEADS), jnp.tile(glb_g_k[j], N_KV_HEADS)]).reshape(1, -1)
            qkv = _qkv_project(xs, mod, g1, glb_w_qkv[j].astype(BF16), jnp.zeros((1, nq + 2 * nk), F32), cos, sin,
                               hg, n_ctx=n_ctx, qk_norm=True)
            o = _attention(qkv, jnp.zeros((N_HEADS,), F32), windowed=False, n_ctx=n_ctx)
            outs = _mixer_out((o,), xs, mod, glb_w_o[j].astype(BF16), zero_bias, *route_args, n_ctx=n_ctx)
        else:
            n_main = 2 * GDN_HEADS * GDN_DK + 2 * GDN_HEADS * LANES
            w_in = _pad_cols(gdn_w_in[j], n_main + LANES).astype(BF16)
            p_main, p_small = _gdn_in_project(xs, mod, g1, w_in, n_ctx=n_ctx, n_main=n_main)
            qkv = _gdn_conv(p_main, gdn_conv_w[j], n_ctx=n_ctx)
            col_pack, row_pack = _gdn_gates(p_small, gdn_a_log[j], gdn_dt_bias[j])
            o_f, o_b = _gdn_delta(qkv, col_pack, row_pack, n_ctx=n_ctx)
            outs = _mixer_out((o_f, o_b, p_main), xs, mod, gdn_w_o[j].astype(BF16), zero_bias, *route_args,
                              n_ctx=n_ctx, gdn_gain=gdn_g_out[j].reshape(1, LANES))
        x_mid, h2, idx, gates = outs
        xs, moe_buf = _moe_layer(x_mid, h2, idx, gates, mod, moe_w_up, moe_b_up[i], moe_w_down, moe_b_down[i],
                                 moe_buf, n_ctx=n_ctx, layer=i,
                                 final_gain=g_final.reshape(1, d) if i == depth - 1 else None)
    return xs
```
